```python
import jax, jax.numpy as jnp
from jax import lax
import numpy as np

D_MODEL = 2048
BATCH = 2
SEQ = 8192
DEPTH = 2

D_MIX = D_MODEL
D_ATTN = D_MIX // 2
D_CONV = D_MIX // 4
D_POOL = D_MIX // 4
HEAD_DIM = 64
ATTN_HEADS = D_ATTN // HEAD_DIM
CONV_GROUPS = 8
CONV_WIDTH = 3
POOL_WINDOWS = (2, 4, 8, 16)
POOL_GROUP = D_POOL // len(POOL_WINDOWS)
DILATED_PATTERNS = ((128, 1), (512, 4), (2048, 16))
BLK = 128
D_IN = 3 * D_ATTN + 3 * D_CONV + D_POOL
D_FF = 5632
FFN_RESIDUAL = 0.5
RMS_EPS = 1e-6
NEG_INF = -1e30

kernel_name = "hybrid_dilated_conv_pool_macaron"


def rmsnorm(x, g):
    xf = x.astype(jnp.float32)
    y = xf * lax.rsqrt(jnp.mean(xf * xf, axis=-1, keepdims=True) + RMS_EPS)
    return (y * g.astype(jnp.float32)).astype(x.dtype)


def swiglu(h, w_gate, w_up, w_down):
    return (jax.nn.silu(h @ w_gate) * (h @ w_up)) @ w_down


def dilated_band_attention(q, k, v, window, dilation):
    B, S, H, Dh = q.shape
    d = dilation
    L = S // d
    span = window // d
    assert span <= BLK
    nb = -(-L // BLK)
    Lp = nb * BLK

    def to_sub(t):
        t = t.reshape(B, L, d, H, Dh).transpose(0, 2, 1, 3, 4).reshape(B * d, L, H, Dh)
        t = jnp.pad(t, ((0, 0), (0, Lp - L), (0, 0), (0, 0)))
        return t.reshape(B * d, nb, BLK, H, Dh)

    def with_prev(t):
        prev = jnp.pad(t, ((0, 0), (1, 0), (0, 0), (0, 0), (0, 0)))[:, :-1]
        return jnp.concatenate([prev, t], axis=2)

    qb = to_sub(q)
    kc = with_prev(to_sub(k))
    vc = with_prev(to_sub(v))
    s = jnp.einsum('znqhd,znkhd->znhqk', qb, kc).astype(jnp.float32) * (Dh ** -0.5)
    qi = jnp.arange(BLK)[:, None]
    kj = jnp.arange(2 * BLK)[None, :]
    dist = qi + BLK - kj
    blk = jnp.arange(nb)[:, None, None]
    valid = (dist >= 0) & (dist <= span) & (blk * BLK + kj - BLK >= 0)
    s = jnp.where(valid[None, :, None], s, NEG_INF)
    m = jnp.max(s, axis=-1, keepdims=True)
    p = jnp.exp(s - m)
    l = jnp.sum(p, axis=-1)
    num = jnp.einsum('znhqk,znkhd->znqhd', p.astype(v.dtype), vc).astype(jnp.float32)
    l_q = l.transpose(0, 1, 3, 2)
    o = num / l_q[..., None]
    lse = m[..., 0].transpose(0, 1, 3, 2) + jnp.log(l_q)

    def from_sub(t):
        rest = t.shape[3:]
        t = t.reshape((B * d, Lp) + rest)[:, :L]
        t = t.reshape((B, d, L) + rest)
        t = jnp.moveaxis(t, 1, 2)
        return t.reshape((B, S) + rest)

    return from_sub(o), from_sub(lse)


def dilated_attention(q, k, v):
    outs, lses = [], []
    for window, dilation in DILATED_PATTERNS:
        o, lse = dilated_band_attention(q, k, v, window, dilation)
        outs.append(o)
        lses.append(lse)
    w = jax.nn.softmax(jnp.stack(lses, axis=0), axis=0)
    o = jnp.sum(w[..., None] * jnp.stack(outs, axis=0), axis=0)
    return o.astype(q.dtype)


def causal_short_conv(u, conv_w):
    S = u.shape[1]
    up = jnp.pad(u, ((0, 0), (CONV_WIDTH - 1, 0), (0, 0)))
    return conv_w[0] * up[:, 0:S] + conv_w[1] * up[:, 1:S + 1] + conv_w[2] * up[:, 2:S + 2]


def multiscale_pool(u, pool_w, pool_scale):
    B, S, C = u.shape
    uf = u.astype(jnp.float32)
    csz = jnp.pad(jnp.cumsum(uf, axis=1), ((0, 0), (1, 0), (0, 0)))
    pos = jnp.arange(S)
    outs = []
    for g, w in enumerate(POOL_WINDOWS):
        sl = slice(g * POOL_GROUP, (g + 1) * POOL_GROUP)
        P = jnp.pad(csz[..., sl], ((0, 0), (w - 1, 0), (0, 0)))
        win_sum = P[:, w:w + S] - P[:, 0:S]
        count = jnp.minimum(pos + 1, w).astype(jnp.float32)[None, :, None]
        outs.append(win_sum / count - uf[..., sl])
    pooled = jnp.stack(outs, axis=2).astype(u.dtype)
    y = jnp.einsum('bsgc,gcd->bsgd', pooled, pool_w).reshape(B, S, C)
    return y * pool_scale


def hybrid_mixer(h, w_in, conv_w, pool_w, pool_scale, w_out):
    B, S, _ = h.shape
    z = h @ w_in
    cuts = np.cumsum([D_ATTN, D_ATTN, D_ATTN, D_CONV, D_CONV, D_CONV])
    q, k, v, gate_b, gate_c, conv_in, pool_in = jnp.split(z, cuts, axis=-1)
    heads = lambda t: t.reshape(B, S, ATTN_HEADS, HEAD_DIM)
    y_attn = dilated_attention(heads(q), heads(k), heads(v)).reshape(B, S, D_ATTN)
    y_conv = gate_b * causal_short_conv(gate_c * conv_in, conv_w)
    y_pool = multiscale_pool(pool_in, pool_w, pool_scale)
    return jnp.concatenate([y_attn, y_conv, y_pool], axis=-1) @ w_out


def setup_inputs(seed: int = 0) -> dict:
    key = jax.random.key(seed)
    ks = jax.random.split(key, 16)
    f32 = jnp.float32

    def lin(k, shape, fan_in):
        return jax.random.normal(k, shape, f32) * (fan_in ** -0.5)

    def gain(k, shape, noise=0.02):
        return 1.0 + noise * jax.random.normal(k, shape, f32)

    return {
        "x": jax.random.normal(ks[0], (BATCH, SEQ, D_MODEL), f32),
        "ffn1_norm": gain(ks[1], (DEPTH, D_MODEL)),
        "ffn1_w_gate": lin(ks[2], (DEPTH, D_MODEL, D_FF), D_MODEL),
        "ffn1_w_up": lin(ks[3], (DEPTH, D_MODEL, D_FF), D_MODEL),
        "ffn1_w_down": lin(ks[4], (DEPTH, D_FF, D_MODEL), D_FF),
        "mix_norm": gain(ks[5], (DEPTH, D_MODEL)),
        "w_in": lin(ks[6], (DEPTH, D_MODEL, D_IN), D_MODEL),
        "conv_w": lin(ks[7], (DEPTH, CONV_WIDTH, D_CONV), CONV_WIDTH),
        "pool_w": lin(ks[8], (DEPTH, len(POOL_WINDOWS), POOL_GROUP, POOL_GROUP), POOL_GROUP),
        "pool_scale": gain(ks[9], (DEPTH, D_POOL), 0.1),
        "w_out": lin(ks[10], (DEPTH, D_MIX, D_MODEL), D_MIX),
        "ffn2_norm": gain(ks[11], (DEPTH, D_MODEL)),
        "ffn2_w_gate": lin(ks[12], (DEPTH, D_MODEL, D_FF), D_MODEL),
        "ffn2_w_up": lin(ks[13], (DEPTH, D_MODEL, D_FF), D_MODEL),
        "ffn2_w_down": lin(ks[14], (DEPTH, D_FF, D_MODEL), D_FF),
        "final_norm": gain(ks[15], (D_MODEL,)),
    }


def reference(x, ffn1_norm, ffn1_w_gate, ffn1_w_up, ffn1_w_down, mix_norm, w_in, conv_w,
              pool_w, pool_scale, w_out, ffn2_norm, ffn2_w_gate, ffn2_w_up, ffn2_w_down,
              final_norm):
    for l in range(DEPTH):
        x = x + FFN_RESIDUAL * swiglu(rmsnorm(x, ffn1_norm[l]), ffn1_w_gate[l], ffn1_w_up[l], ffn1_w_down[l])
        x = x + hybrid_mixer(rmsnorm(x, mix_norm[l]), w_in[l], conv_w[l], pool_w[l], pool_scale[l], w_out[l])
        x = x + FFN_RESIDUAL * swiglu(rmsnorm(x, ffn2_norm[l]), ffn2_w_gate[l], ffn2_w_up[l], ffn2_w_down[l])
    return rmsnorm(x, final_norm)
```

```python
import functools

import jax
import jax.numpy as jnp
from jax import lax
from jax.experimental import pallas as pl
from jax.experimental.pallas import tpu as pltpu

D_MODEL = 2048
D_FF = 5632
D_ATTN = 1024
D_CONV = 512
D_POOL = 512
D_IN = 3 * D_ATTN + 3 * D_CONV + D_POOL
D_CP = D_CONV + D_POOL
HEAD_DIM = 64
CONV_WIDTH = 3
POOL_WINDOWS = (2, 4, 8, 16)
POOL_GROUP = D_POOL // len(POOL_WINDOWS)
DILATIONS = (1, 4, 16)
BLK = 128
RMS_EPS = 1e-6
NEG_INF = -1e30
FFN_RESIDUAL = 0.5

LANES = 128
HALO = 16
SUPER = BLK * max(DILATIONS)
HEADS_PER_STEP = LANES // HEAD_DIM

FFN_TM, FFN_TF = 1024, 512
INPROJ_TM = 512
OUTPROJ_TM = 1024
VMEM_LIMIT = 60 * 1024 * 1024

F32 = jnp.float32
BF16 = jnp.bfloat16


def _rmsnorm(x, g):
    return x * lax.rsqrt(jnp.mean(x * x, axis=-1, keepdims=True) + RMS_EPS) * g


def _ffn_kernel(*refs, final):
    if final:
        x_ref, g_ref, wg_ref, wu_ref, wd_ref, fg_ref, o_ref, h_ref = refs
    else:
        x_ref, g_ref, wg_ref, wu_ref, wd_ref, o_ref, h_ref = refs
    j = pl.program_id(1)

    @pl.when(j == 0)
    def _():
        x = x_ref[...]
        h_ref[...] = _rmsnorm(x, g_ref[...]).astype(BF16)
        o_ref[...] = x

    h = h_ref[...]
    gate = jnp.dot(h, wg_ref[...], preferred_element_type=F32)
    up = jnp.dot(h, wu_ref[...], preferred_element_type=F32)
    a = (gate * jax.nn.sigmoid(gate) * up * FFN_RESIDUAL).astype(BF16)
    o_ref[...] += jnp.dot(a, wd_ref[...], preferred_element_type=F32)

    if final:
        @pl.when(j == pl.num_programs(1) - 1)
        def _():
            o_ref[...] = _rmsnorm(o_ref[...], fg_ref[...])


def _ffn(x, g, wg, wu, wd, final_g=None):
    t = x.shape[0]
    final = final_g is not None
    in_specs = [
        pl.BlockSpec((FFN_TM, D_MODEL), lambda i, j: (i, 0), pipeline_mode=pl.Buffered(1)),
        pl.BlockSpec((1, D_MODEL), lambda i, j: (0, 0)),
        pl.BlockSpec((D_MODEL, FFN_TF), lambda i, j: (0, j)),
        pl.BlockSpec((D_MODEL, FFN_TF), lambda i, j: (0, j)),
        pl.BlockSpec((FFN_TF, D_MODEL), lambda i, j: (j, 0)),
    ]
    args = [x, g.reshape(1, D_MODEL), wg, wu, wd]
    if final:
        in_specs.append(pl.BlockSpec((1, D_MODEL), lambda i, j: (0, 0)))
        args.append(final_g.reshape(1, D_MODEL))
    return pl.pallas_call(
        functools.partial(_ffn_kernel, final=final),
        grid=(t // FFN_TM, D_FF // FFN_TF),
        in_specs=in_specs,
        out_specs=pl.BlockSpec((FFN_TM, D_MODEL), lambda i, j: (i, 0)),
        out_shape=jax.ShapeDtypeStruct((t, D_MODEL), F32),
        scratch_shapes=[pltpu.VMEM((FFN_TM, D_MODEL), BF16)],
        compiler_params=pltpu.CompilerParams(
            dimension_semantics=("parallel", "arbitrary"), vmem_limit_bytes=VMEM_LIMIT),
        name="ffn_final" if final else "ffn",
    )(*args)


def _inproj_kernel(x_ref, g_ref, w_ref, cw_ref, pw_ref, ps_ref, qkv_ref, ycp_ref,
                   ubuf, pbuf, *, tiles_per_seq):
    tm = x_ref.shape[0]
    i = pl.program_id(0)
    h = _rmsnorm(x_ref[...], g_ref[...]).astype(BF16)

    for c in range(3):
        cols = slice(c * D_ATTN, (c + 1) * D_ATTN)
        qkv_ref[:, cols] = jnp.dot(h, w_ref[:, cols], preferred_element_type=F32)

    c0 = 3 * D_ATTN
    gate_b = jnp.dot(h, w_ref[:, c0:c0 + D_CONV], preferred_element_type=F32)
    gate_c = jnp.dot(h, w_ref[:, c0 + D_CONV:c0 + 2 * D_CONV], preferred_element_type=F32)
    conv_in = jnp.dot(h, w_ref[:, c0 + 2 * D_CONV:c0 + 3 * D_CONV], preferred_element_type=F32)
    pool_in = jnp.dot(h, w_ref[:, c0 + 3 * D_CONV:], preferred_element_type=F32)
    u = gate_c * conv_in

    @pl.when(i % tiles_per_seq == 0)
    def _():
        ubuf[0:HALO, :] = jnp.zeros((HALO, D_CONV), F32)
        pbuf[0:HALO, :] = jnp.zeros((HALO, D_POOL), F32)

    ubuf[HALO:HALO + tm, :] = u
    pbuf[HALO:HALO + tm, :] = pool_in

    conv = cw_ref[CONV_WIDTH - 1:CONV_WIDTH, :] * u
    for k in range(1, CONV_WIDTH):
        conv += cw_ref[CONV_WIDTH - 1 - k:CONV_WIDTH - k, :] * ubuf[HALO - k:HALO - k + tm, :]
    ycp_ref[:, 0:D_CONV] = (gate_b * conv).astype(BF16)

    pos = (i % tiles_per_seq) * tm + lax.broadcasted_iota(jnp.int32, (tm, 1), 0)
    for gidx, w in enumerate(POOL_WINDOWS):
        cols = slice(gidx * POOL_GROUP, (gidx + 1) * POOL_GROUP)
        cur = pool_in[:, cols]
        win = cur
        for k in range(1, w):
            win += pbuf[HALO - k:HALO - k + tm, cols]
        count = jnp.minimum(pos + 1, w).astype(F32)
        pooled = (win / count - cur).astype(BF16)
        y = jnp.dot(pooled, pw_ref[gidx], preferred_element_type=F32) * ps_ref[:, cols]
        ycp_ref[:, D_CONV + gidx * POOL_GROUP:D_CONV + (gidx + 1) * POOL_GROUP] = y.astype(BF16)

    ubuf[0:HALO, :] = ubuf[tm:tm + HALO, :]
    pbuf[0:HALO, :] = pbuf[tm:tm + HALO, :]


def _inproj(x, g, w_in, conv_w, pool_w, pool_scale, seq):
    t = x.shape[0]
    tm = INPROJ_TM
    const = lambda *shape: pl.BlockSpec(shape, lambda i: (0,) * len(shape),
                                        pipeline_mode=pl.Buffered(1))
    return pl.pallas_call(
        functools.partial(_inproj_kernel, tiles_per_seq=seq // tm),
        grid=(t // tm,),
        in_specs=[
            pl.BlockSpec((tm, D_MODEL), lambda i: (i, 0)),
            const(1, D_MODEL),
            const(D_MODEL, D_IN),
            const(CONV_WIDTH, D_CONV),
            const(len(POOL_WINDOWS), POOL_GROUP, POOL_GROUP),
            const(1, D_POOL),
        ],
        out_specs=[
            pl.BlockSpec((tm, 3 * D_ATTN), lambda i: (i, 0)),
            pl.BlockSpec((tm, D_CP), lambda i: (i, 0)),
        ],
        out_shape=[
            jax.ShapeDtypeStruct((t, 3 * D_ATTN), F32),
            jax.ShapeDtypeStruct((t, D_CP), BF16),
        ],
        scratch_shapes=[pltpu.VMEM((HALO + tm, D_CONV), F32),
                        pltpu.VMEM((HALO + tm, D_POOL), F32)],
        compiler_params=pltpu.CompilerParams(
            dimension_semantics=("arbitrary",), vmem_limit_bytes=VMEM_LIMIT),
        name="inproj",
    )(x, g.reshape(1, D_MODEL), w_in, conv_w, pool_w, pool_scale.reshape(1, D_POOL))


def _attn_kernel(q_ref, kp_ref, kc_ref, vp_ref, vc_ref, o_ref, acc_ref, m_ref, l_ref):
    n = pl.program_id(2)
    lane = lax.broadcasted_iota(jnp.int32, (BLK, LANES), 1)
    head_a = lane < HEAD_DIM
    qi = lax.broadcasted_iota(jnp.int32, (BLK, BLK), 0)
    kj = lax.broadcasted_iota(jnp.int32, (BLK, BLK), 1)
    mask_hi = kj <= qi
    mask_lo = kj >= qi
    mask_lo_prev = jnp.logical_and(mask_lo, n > 0)
    dims_nt = (((1,), (1,)), ((), ()))

    def rows(off, d):
        return pl.ds(off, BLK) if d == 1 else pl.ds(off, BLK, stride=d)

    def combo(q_off, lo_off, d, from_prev, first):
        qr = rows(q_off, d)
        lr = rows(lo_off, d)
        q = q_ref[qr, :]
        k_hi = kc_ref[qr, :].astype(BF16)
        v_hi = vc_ref[qr, :].astype(BF16)
        k_lo = (kp_ref if from_prev else kc_ref)[lr, :].astype(BF16)
        v_lo = (vp_ref if from_prev else vc_ref)[lr, :].astype(BF16)
        m_lo = mask_lo_prev if from_prev else mask_lo
        pvs, alphas = [], []
        for hd in range(HEADS_PER_STEP):
            keep = head_a if hd == 0 else jnp.logical_not(head_a)
            qh = jnp.where(keep, q, 0.0).astype(BF16)
            s_lo = lax.dot_general(qh, k_lo, dims_nt, preferred_element_type=F32) * (HEAD_DIM ** -0.5)
            s_hi = lax.dot_general(qh, k_hi, dims_nt, preferred_element_type=F32) * (HEAD_DIM ** -0.5)
            s_lo = jnp.where(m_lo, s_lo, NEG_INF)
            s_hi = jnp.where(mask_hi, s_hi, NEG_INF)
            mx = jnp.maximum(jnp.max(s_lo, axis=1, keepdims=True),
                             jnp.max(s_hi, axis=1, keepdims=True))
            mx = jnp.broadcast_to(mx, (BLK, LANES))
            if first:
                m_new = mx
            else:
                m_old = m_ref[hd, qr, :]
                m_new = jnp.maximum(m_old, mx)
            p_lo = jnp.exp(s_lo - m_new)
            p_hi = jnp.exp(s_hi - m_new)
            rs = jnp.sum(p_lo, axis=1, keepdims=True) + jnp.sum(p_hi, axis=1, keepdims=True)
            rs = jnp.broadcast_to(rs, (BLK, LANES))
            pv = (jnp.dot(p_lo.astype(BF16), v_lo, preferred_element_type=F32)
                  + jnp.dot(p_hi.astype(BF16), v_hi, preferred_element_type=F32))
            if first:
                l_new = rs
            else:
                alpha = jnp.exp(m_old - m_new)
                l_new = alpha * l_ref[hd, qr, :] + rs
                alphas.append(alpha)
            m_ref[hd, qr, :] = m_new
            l_ref[hd, qr, :] = l_new
            pvs.append(pv)
        pv = jnp.where(head_a, pvs[0], pvs[1])
        if first:
            acc_ref[qr, :] = pv
        else:
            acc_ref[qr, :] = jnp.where(head_a, alphas[0], alphas[1]) * acc_ref[qr, :] + pv

    for d in DILATIONS:
        first = d == DILATIONS[0]
        nb = SUPER // (BLK * d)

        def block0(r, carry, d=d, first=first):
            combo(r, r + SUPER - BLK * d, d, True, first)
            return carry

        def later(idx, carry, d=d, first=first, nb=nb):
            r = idx // (nb - 1)
            b = idx % (nb - 1) + 1
            q_off = r + d * BLK * b
            if d == 1:
                q_off = pl.multiple_of(q_off, BLK)
            combo(q_off, q_off - BLK * d, d, False, first)
            return carry

        lax.fori_loop(0, d, block0, 0)
        if nb > 1:
            lax.fori_loop(0, d * (nb - 1), later, 0)

    l_all = jnp.where(lax.broadcasted_iota(jnp.int32, (SUPER, LANES), 1) < HEAD_DIM,
                      l_ref[0], l_ref[1])
    o_ref[...] = (acc_ref[...] / l_all).astype(o_ref.dtype)


def _attn(qkv, batch, seq):
    ns = seq // SUPER
    hg = D_ATTN // LANES
    blk = lambda imap: pl.BlockSpec((None, SUPER, LANES), imap)
    return pl.pallas_call(
        _attn_kernel,
        grid=(batch, hg, ns),
        in_specs=[
            blk(lambda b, g, n: (b, n, g)),
            blk(lambda b, g, n: (b, jnp.maximum(n - 1, 0), hg + g)),
            blk(lambda b, g, n: (b, n, hg + g)),
            blk(lambda b, g, n: (b, jnp.maximum(n - 1, 0), 2 * hg + g)),
            blk(lambda b, g, n: (b, n, 2 * hg + g)),
        ],
        out_specs=blk(lambda b, g, n: (b, n, g)),
        out_shape=jax.ShapeDtypeStruct((batch, seq, D_ATTN), BF16),
        scratch_shapes=[pltpu.VMEM((SUPER, LANES), F32),
                        pltpu.VMEM((HEADS_PER_STEP, SUPER, LANES), F32),
                        pltpu.VMEM((HEADS_PER_STEP, SUPER, LANES), F32)],
        compiler_params=pltpu.CompilerParams(
            dimension_semantics=("parallel", "parallel", "arbitrary"),
            vmem_limit_bytes=VMEM_LIMIT),
        name="attn",
    )(qkv, qkv, qkv, qkv, qkv)


def _outproj_kernel(x_ref, a_ref, cp_ref, w_ref, o_ref):
    o_ref[...] = (x_ref[...]
                  + jnp.dot(a_ref[...], w_ref[0:D_ATTN, :], preferred_element_type=F32)
                  + jnp.dot(cp_ref[...], w_ref[D_ATTN:, :], preferred_element_type=F32))


def _outproj(x, y_attn, y_cp, w_out):
    t = x.shape[0]
    tm = OUTPROJ_TM
    return pl.pallas_call(
        _outproj_kernel,
        grid=(t // tm,),
        in_specs=[
            pl.BlockSpec((tm, D_MODEL), lambda i: (i, 0)),
            pl.BlockSpec((tm, D_ATTN), lambda i: (i, 0)),
            pl.BlockSpec((tm, D_CP), lambda i: (i, 0)),
            pl.BlockSpec((D_MODEL, D_MODEL), lambda i: (0, 0), pipeline_mode=pl.Buffered(1)),
        ],
        out_specs=pl.BlockSpec((tm, D_MODEL), lambda i: (i, 0)),
        out_shape=jax.ShapeDtypeStruct((t, D_MODEL), F32),
        compiler_params=pltpu.CompilerParams(
            dimension_semantics=("parallel",), vmem_limit_bytes=VMEM_LIMIT),
        name="outproj",
    )(x, y_attn, y_cp, w_out)


def kernel(x, ffn1_norm, ffn1_w_gate, ffn1_w_up, ffn1_w_down, mix_norm, w_in, conv_w, pool_w, pool_scale, w_out, ffn2_norm, ffn2_w_gate, ffn2_w_up, ffn2_w_down, final_norm):
    batch, seq, _ = x.shape
    depth = w_in.shape[0]
    assert seq % SUPER == 0 and seq % INPROJ_TM == 0 and (batch * seq) % FFN_TM == 0
    t = batch * seq
    x = x.reshape(t, D_MODEL)
    bf = lambda w: w.astype(BF16)
    for l in range(depth):
        x = _ffn(x, ffn1_norm[l], bf(ffn1_w_gate[l]), bf(ffn1_w_up[l]), bf(ffn1_w_down[l]))
        qkv, y_cp = _inproj(x, mix_norm[l], bf(w_in[l]), conv_w[l], bf(pool_w[l]),
                            pool_scale[l], seq)
        y_attn = _attn(qkv.reshape(batch, seq, 3 * D_ATTN), batch, seq)
        x = _outproj(x, y_attn.reshape(t, D_ATTN), y_cp, bf(w_out[l]))
        x = _ffn(x, ffn2_norm[l], bf(ffn2_w_gate[l]), bf(ffn2_w_up[l]), bf(ffn2_w_down[l]),
                 final_g=final_norm if l == depth - 1 else None)
    return x.reshape(batch, seq, D_MODEL)
```

```python
import functools

import jax
import jax.numpy as jnp
from jax import lax
from jax.experimental import pallas as pl
from jax.experimental.pallas import tpu as pltpu

D_MODEL = 2048
D_FF = 5632
D_ATTN = 1024
D_CONV = 512
D_POOL = 512
D_IN = 3 * D_ATTN + 3 * D_CONV + D_POOL
D_CP = D_CONV + D_POOL
HEAD_DIM = 64
CONV_WIDTH = 3
POOL_WINDOWS = (2, 4, 8, 16)
POOL_GROUP = D_POOL // len(POOL_WINDOWS)
DILATIONS = (1, 4, 16)
BLK = 128
RMS_EPS = 1e-6
NEG_INF = -1e30
FFN_RESIDUAL = 0.5

LANES = 128
HALO = 16
SUPER = BLK * max(DILATIONS)
HEADS_PER_STEP = LANES // HEAD_DIM
ATTN_GROUP = 4

FFN_TM, FFN_TF = 1024, 512
INPROJ_TM = 512
OUTPROJ_TM = 1024
VMEM_LIMIT = 60 * 1024 * 1024

F32 = jnp.float32
BF16 = jnp.bfloat16


def _rmsnorm(x, g):
    return x * lax.rsqrt(jnp.mean(x * x, axis=-1, keepdims=True) + RMS_EPS) * g


def _ffn_kernel(*refs, final):
    if final:
        x_ref, g_ref, wg_ref, wu_ref, wd_ref, fg_ref, o_ref, h_ref = refs
    else:
        x_ref, g_ref, wg_ref, wu_ref, wd_ref, o_ref, h_ref = refs
    j = pl.program_id(1)

    @pl.when(j == 0)
    def _():
        x = x_ref[...]
        h_ref[...] = _rmsnorm(x, g_ref[...]).astype(BF16)
        o_ref[...] = x

    h = h_ref[...]
    gate = jnp.dot(h, wg_ref[...], preferred_element_type=F32)
    up = jnp.dot(h, wu_ref[...], preferred_element_type=F32)
    a = (gate * jax.nn.sigmoid(gate) * up * FFN_RESIDUAL).astype(BF16)
    o_ref[...] += jnp.dot(a, wd_ref[...], preferred_element_type=F32)

    if final:
        @pl.when(j == pl.num_programs(1) - 1)
        def _():
            o_ref[...] = _rmsnorm(o_ref[...], fg_ref[...])


def _ffn(x, g, wg, wu, wd, final_g=None):
    t = x.shape[0]
    final = final_g is not None
    in_specs = [
        pl.BlockSpec((FFN_TM, D_MODEL), lambda i, j: (i, 0), pipeline_mode=pl.Buffered(1)),
        pl.BlockSpec((1, D_MODEL), lambda i, j: (0, 0)),
        pl.BlockSpec((D_MODEL, FFN_TF), lambda i, j: (0, j)),
        pl.BlockSpec((D_MODEL, FFN_TF), lambda i, j: (0, j)),
        pl.BlockSpec((FFN_TF, D_MODEL), lambda i, j: (j, 0)),
    ]
    args = [x, g.reshape(1, D_MODEL), wg, wu, wd]
    if final:
        in_specs.append(pl.BlockSpec((1, D_MODEL), lambda i, j: (0, 0)))
        args.append(final_g.reshape(1, D_MODEL))
    return pl.pallas_call(
        functools.partial(_ffn_kernel, final=final),
        grid=(t // FFN_TM, D_FF // FFN_TF),
        in_specs=in_specs,
        out_specs=pl.BlockSpec((FFN_TM, D_MODEL), lambda i, j: (i, 0)),
        out_shape=jax.ShapeDtypeStruct((t, D_MODEL), F32),
        scratch_shapes=[pltpu.VMEM((FFN_TM, D_MODEL), BF16)],
        compiler_params=pltpu.CompilerParams(
            dimension_semantics=("parallel", "arbitrary"), vmem_limit_bytes=VMEM_LIMIT),
        name="ffn_final" if final else "ffn",
    )(*args)


def _inproj_kernel(x_ref, g_ref, w_ref, cw_ref, pw_ref, ps_ref, qkv_ref, ycp_ref,
                   ubuf, pbuf, *, tiles_per_seq):
    tm = x_ref.shape[0]
    i = pl.program_id(0)
    h = _rmsnorm(x_ref[...], g_ref[...]).astype(BF16)

    for c in range(3):
        cols = slice(c * D_ATTN, (c + 1) * D_ATTN)
        qkv_ref[:, cols] = jnp.dot(h, w_ref[:, cols], preferred_element_type=F32)

    c0 = 3 * D_ATTN
    gate_b = jnp.dot(h, w_ref[:, c0:c0 + D_CONV], preferred_element_type=F32)
    gate_c = jnp.dot(h, w_ref[:, c0 + D_CONV:c0 + 2 * D_CONV], preferred_element_type=F32)
    conv_in = jnp.dot(h, w_ref[:, c0 + 2 * D_CONV:c0 + 3 * D_CONV], preferred_element_type=F32)
    pool_in = jnp.dot(h, w_ref[:, c0 + 3 * D_CONV:], preferred_element_type=F32)
    u = gate_c * conv_in

    @pl.when(i % tiles_per_seq == 0)
    def _():
        ubuf[0:HALO, :] = jnp.zeros((HALO, D_CONV), F32)
        pbuf[0:HALO, :] = jnp.zeros((HALO, D_POOL), F32)

    ubuf[HALO:HALO + tm, :] = u
    pbuf[HALO:HALO + tm, :] = pool_in

    conv = cw_ref[CONV_WIDTH - 1:CONV_WIDTH, :] * u
    for k in range(1, CONV_WIDTH):
        conv += cw_ref[CONV_WIDTH - 1 - k:CONV_WIDTH - k, :] * ubuf[HALO - k:HALO - k + tm, :]
    ycp_ref[:, 0:D_CONV] = (gate_b * conv).astype(BF16)

    pos = (i % tiles_per_seq) * tm + lax.broadcasted_iota(jnp.int32, (tm, 1), 0)
    for gidx, w in enumerate(POOL_WINDOWS):
        cols = slice(gidx * POOL_GROUP, (gidx + 1) * POOL_GROUP)
        cur = pool_in[:, cols]
        win = cur
        for k in range(1, w):
            win += pbuf[HALO - k:HALO - k + tm, cols]
        count = jnp.minimum(pos + 1, w).astype(F32)
        pooled = (win / count - cur).astype(BF16)
        y = jnp.dot(pooled, pw_ref[gidx], preferred_element_type=F32) * ps_ref[:, cols]
        ycp_ref[:, D_CONV + gidx * POOL_GROUP:D_CONV + (gidx + 1) * POOL_GROUP] = y.astype(BF16)

    ubuf[0:HALO, :] = ubuf[tm:tm + HALO, :]
    pbuf[0:HALO, :] = pbuf[tm:tm + HALO, :]


def _inproj(x, g, w_in, conv_w, pool_w, pool_scale, seq):
    t = x.shape[0]
    tm = INPROJ_TM
    const = lambda *shape: pl.BlockSpec(shape, lambda i: (0,) * len(shape),
                                        pipeline_mode=pl.Buffered(1))
    return pl.pallas_call(
        functools.partial(_inproj_kernel, tiles_per_seq=seq // tm),
        grid=(t // tm,),
        in_specs=[
            pl.BlockSpec((tm, D_MODEL), lambda i: (i, 0)),
            const(1, D_MODEL),
            const(D_MODEL, D_IN),
            const(CONV_WIDTH, D_CONV),
            const(len(POOL_WINDOWS), POOL_GROUP, POOL_GROUP),
            const(1, D_POOL),
        ],
        out_specs=[
            pl.BlockSpec((tm, 3 * D_ATTN), lambda i: (i, 0)),
            pl.BlockSpec((tm, D_CP), lambda i: (i, 0)),
        ],
        out_shape=[
            jax.ShapeDtypeStruct((t, 3 * D_ATTN), F32),
            jax.ShapeDtypeStruct((t, D_CP), BF16),
        ],
        scratch_shapes=[pltpu.VMEM((HALO + tm, D_CONV), F32),
                        pltpu.VMEM((HALO + tm, D_POOL), F32)],
        compiler_params=pltpu.CompilerParams(
            dimension_semantics=("arbitrary",), vmem_limit_bytes=VMEM_LIMIT),
        name="inproj",
    )(x, g.reshape(1, D_MODEL), w_in, conv_w, pool_w, pool_scale.reshape(1, D_POOL))


def _attn_kernel(q_ref, kp_ref, kc_ref, vp_ref, vc_ref, o_ref, num_ref, m_ref, l_ref):
    n = pl.program_id(2)
    head_a = lax.broadcasted_iota(jnp.int32, (BLK, LANES), 1) < HEAD_DIM
    qi = lax.broadcasted_iota(jnp.int32, (BLK, 2 * BLK), 0)
    kj = lax.broadcasted_iota(jnp.int32, (BLK, 2 * BLK), 1)
    band = jnp.logical_and(kj >= qi, kj <= qi + BLK)
    band_prev = jnp.logical_and(band, jnp.logical_or(kj >= BLK, n > 0))
    dims_nt = (((1,), (1,)), ((), ()))

    def rows(off, size, d):
        return pl.ds(off, size) if d == 1 else pl.ds(off, size, stride=d)

    def combo(pat, d, q_off, lo_off, from_prev):
        qr = rows(q_off, BLK, d)
        if from_prev:
            lr = rows(lo_off, BLK, d)
            k = jnp.concatenate([kp_ref[lr, :], kc_ref[qr, :]], axis=0)
            v = jnp.concatenate([vp_ref[lr, :], vc_ref[qr, :]], axis=0)
        else:
            kr = rows(lo_off, 2 * BLK, d)
            k = kc_ref[kr, :]
            v = vc_ref[kr, :]
        k = k.astype(BF16)
        v = v.astype(BF16)
        q = (q_ref[qr, :] * (HEAD_DIM ** -0.5)).astype(BF16)
        zero = jnp.zeros_like(q)
        q2 = jnp.concatenate([jnp.where(head_a, q, zero), jnp.where(head_a, zero, q)], axis=0)
        s2 = lax.dot_general(q2, k, dims_nt, preferred_element_type=F32)
        valid = band_prev if from_prev else band
        ps, ms, ls = [], [], []
        for hd in range(HEADS_PER_STEP):
            s = jnp.where(valid, s2[hd * BLK:(hd + 1) * BLK, :], NEG_INF)
            m = jnp.max(jnp.maximum(s[:, :BLK], s[:, BLK:]), axis=1, keepdims=True)
            p = jnp.exp(s - m)
            l = jnp.sum(p[:, :BLK] + p[:, BLK:], axis=1, keepdims=True)
            ps.append(p.astype(BF16))
            ms.append(jnp.broadcast_to(m, (BLK, LANES)))
            ls.append(jnp.broadcast_to(l, (BLK, LANES)))
        pv2 = jnp.dot(jnp.concatenate(ps, axis=0), v, preferred_element_type=F32)
        num_ref[pat, qr, :] = jnp.where(head_a, pv2[:BLK], pv2[BLK:])
        m_ref[pat, qr, :] = jnp.where(head_a, ms[0], ms[1])
        l_ref[pat, qr, :] = jnp.where(head_a, ls[0], ls[1])

    def loop(trips, body):
        def wrapped(it, carry):
            body(it)
            return carry
        lax.fori_loop(0, trips, wrapped, 0)

    for pat, d in enumerate(DILATIONS):
        nb = SUPER // (BLK * d)
        first_group = min(d, ATTN_GROUP)

        def block0(it, pat=pat, d=d, g=first_group):
            for u in range(g):
                r = it * g + u
                combo(pat, d, r, r + SUPER - BLK * d, True)

        if d == first_group:
            block0(0)
        else:
            loop(d // first_group, block0)

        if nb > 1:
            g = next(c for c in (ATTN_GROUP, 3, 2, 1) if (nb - 1) % c == 0)

            def later(it, pat=pat, d=d, nb=nb, g=g):
                r = it // ((nb - 1) // g)
                b0 = (it % ((nb - 1) // g)) * g + 1
                for u in range(g):
                    q_off = r + d * BLK * (b0 + u)
                    if d == 1:
                        q_off = pl.multiple_of(q_off, BLK)
                    combo(pat, d, q_off, q_off - BLK * d, False)

            loop(d * (nb - 1) // g, later)

    m_all = m_ref[...]
    w = jnp.exp(m_all - jnp.max(m_all, axis=0)[None])
    den = jnp.sum(w * l_ref[...], axis=0)
    o_ref[...] = (jnp.sum(w * num_ref[...], axis=0) / den).astype(o_ref.dtype)


def _attn(qkv, batch, seq):
    ns = seq // SUPER
    hg = D_ATTN // LANES
    blk = lambda imap: pl.BlockSpec((None, SUPER, LANES), imap)
    return pl.pallas_call(
        _attn_kernel,
        grid=(batch, hg, ns),
        in_specs=[
            blk(lambda b, g, n: (b, n, g)),
            blk(lambda b, g, n: (b, jnp.maximum(n - 1, 0), hg + g)),
            blk(lambda b, g, n: (b, n, hg + g)),
            blk(lambda b, g, n: (b, jnp.maximum(n - 1, 0), 2 * hg + g)),
            blk(lambda b, g, n: (b, n, 2 * hg + g)),
        ],
        out_specs=blk(lambda b, g, n: (b, n, g)),
        out_shape=jax.ShapeDtypeStruct((batch, seq, D_ATTN), BF16),
        scratch_shapes=[pltpu.VMEM((len(DILATIONS), SUPER, LANES), F32)] * 3,
        compiler_params=pltpu.CompilerParams(
            dimension_semantics=("parallel", "parallel", "arbitrary"),
            vmem_limit_bytes=VMEM_LIMIT),
        name="attn",
    )(qkv, qkv, qkv, qkv, qkv)


def _outproj_kernel(x_ref, a_ref, cp_ref, w_ref, o_ref):
    o_ref[...] = (x_ref[...]
                  + jnp.dot(a_ref[...], w_ref[0:D_ATTN, :], preferred_element_type=F32)
                  + jnp.dot(cp_ref[...], w_ref[D_ATTN:, :], preferred_element_type=F32))


def _outproj(x, y_attn, y_cp, w_out):
    t = x.shape[0]
    tm = OUTPROJ_TM
    return pl.pallas_call(
        _outproj_kernel,
        grid=(t // tm,),
        in_specs=[
            pl.BlockSpec((tm, D_MODEL), lambda i: (i, 0)),
            pl.BlockSpec((tm, D_ATTN), lambda i: (i, 0)),
            pl.BlockSpec((tm, D_CP), lambda i: (i, 0)),
            pl.BlockSpec((D_MODEL, D_MODEL), lambda i: (0, 0), pipeline_mode=pl.Buffered(1)),
        ],
        out_specs=pl.BlockSpec((tm, D_MODEL), lambda i: (i, 0)),
        out_shape=jax.ShapeDtypeStruct((t, D_MODEL), F32),
        compiler_params=pltpu.CompilerParams(
            dimension_semantics=("parallel",), vmem_limit_bytes=VMEM_LIMIT),
        name="outproj",
    )(x, y_attn, y_cp, w_out)


def kernel(x, ffn1_norm, ffn1_w_gate, ffn1_w_up, ffn1_w_down, mix_norm, w_in, conv_w, pool_w, pool_scale, w_out, ffn2_norm, ffn2_w_gate, ffn2_w_up, ffn2_w_down, final_norm):
    batch, seq, _ = x.shape
    depth = w_in.shape[0]
    assert seq % SUPER == 0 and seq % INPROJ_TM == 0 and (batch * seq) % FFN_TM == 0
    t = batch * seq
    x = x.reshape(t, D_MODEL)
    bf = lambda w: w.astype(BF16)
    for l in range(depth):
        x = _ffn(x, ffn1_norm[l], bf(ffn1_w_gate[l]), bf(ffn1_w_up[l]), bf(ffn1_w_down[l]))
        qkv, y_cp = _inproj(x, mix_norm[l], bf(w_in[l]), conv_w[l], bf(pool_w[l]),
                            pool_scale[l], seq)
        y_attn = _attn(qkv.reshape(batch, seq, 3 * D_ATTN), batch, seq)
        x = _outproj(x, y_attn.reshape(t, D_ATTN), y_cp, bf(w_out[l]))
        x = _ffn(x, ffn2_norm[l], bf(ffn2_w_gate[l]), bf(ffn2_w_up[l]), bf(ffn2_w_down[l]),
                 final_g=final_norm if l == depth - 1 else None)
    return x.reshape(batch, seq, D_MODEL)
```

```python
import functools

import jax
import jax.numpy as jnp
from jax import lax
from jax.experimental import pallas as pl
from jax.experimental.pallas import tpu as pltpu

D_MODEL = 2048
D_FF = 5632
D_ATTN = 1024
D_CONV = 512
D_POOL = 512
D_IN = 3 * D_ATTN + 3 * D_CONV + D_POOL
D_CP = D_CONV + D_POOL
HEAD_DIM = 64
CONV_WIDTH = 3
POOL_WINDOWS = (2, 4, 8, 16)
POOL_GROUP = D_POOL // len(POOL_WINDOWS)
DILATIONS = (1, 4, 16)
BLK = 128
RMS_EPS = 1e-6
NEG_INF = -1e30
FFN_RESIDUAL = 0.5

LANES = 128
HALO = 16
SUPER = BLK * max(DILATIONS)
HEADS_PER_STEP = LANES // HEAD_DIM

FFN_TM, FFN_TF = 1024, 512
INPROJ_TM = 512
OUTPROJ_TM = 1024
VMEM_LIMIT = 60 * 1024 * 1024

F32 = jnp.float32
BF16 = jnp.bfloat16


def _rmsnorm(x, g):
    return x * lax.rsqrt(jnp.mean(x * x, axis=-1, keepdims=True) + RMS_EPS) * g


def _ffn_kernel(*refs, final):
    if final:
        x_ref, g_ref, wg_ref, wu_ref, wd_ref, fg_ref, o_ref, h_ref = refs
    else:
        x_ref, g_ref, wg_ref, wu_ref, wd_ref, o_ref, h_ref = refs
    j = pl.program_id(1)

    @pl.when(j == 0)
    def _():
        x = x_ref[...]
        h_ref[...] = _rmsnorm(x, g_ref[...]).astype(BF16)
        o_ref[...] = x

    h = h_ref[...]
    gate = jnp.dot(h, wg_ref[...], preferred_element_type=F32)
    up = jnp.dot(h, wu_ref[...], preferred_element_type=F32)
    a = (gate * jax.nn.sigmoid(gate) * up * FFN_RESIDUAL).astype(BF16)
    o_ref[...] += jnp.dot(a, wd_ref[...], preferred_element_type=F32)

    if final:
        @pl.when(j == pl.num_programs(1) - 1)
        def _():
            o_ref[...] = _rmsnorm(o_ref[...], fg_ref[...])


def _ffn(x, layer, g, wg, wu, wd, final_g=None):
    t = x.shape[0]
    final = final_g is not None
    in_specs = [
        pl.BlockSpec((FFN_TM, D_MODEL), lambda i, j: (i, 0), pipeline_mode=pl.Buffered(1)),
        pl.BlockSpec((None, 1, D_MODEL), lambda i, j: (layer, 0, 0)),
        pl.BlockSpec((None, D_MODEL, FFN_TF), lambda i, j: (layer, 0, j)),
        pl.BlockSpec((None, D_MODEL, FFN_TF), lambda i, j: (layer, 0, j)),
        pl.BlockSpec((None, FFN_TF, D_MODEL), lambda i, j: (layer, j, 0)),
    ]
    args = [x, g[:, None, :], wg, wu, wd]
    if final:
        in_specs.append(pl.BlockSpec((1, D_MODEL), lambda i, j: (0, 0)))
        args.append(final_g.reshape(1, D_MODEL))
    return pl.pallas_call(
        functools.partial(_ffn_kernel, final=final),
        grid=(t // FFN_TM, D_FF // FFN_TF),
        in_specs=in_specs,
        out_specs=pl.BlockSpec((FFN_TM, D_MODEL), lambda i, j: (i, 0)),
        out_shape=jax.ShapeDtypeStruct((t, D_MODEL), F32),
        scratch_shapes=[pltpu.VMEM((FFN_TM, D_MODEL), BF16)],
        compiler_params=pltpu.CompilerParams(
            dimension_semantics=("parallel", "arbitrary"), vmem_limit_bytes=VMEM_LIMIT),
        name="ffn_final" if final else "ffn",
    )(*args)


def _inproj_kernel(x_ref, g_ref, w_ref, cw_ref, pw_ref, ps_ref, qkv_ref, ycp_ref,
                   ubuf, pbuf, *, tiles_per_seq):
    tm = x_ref.shape[0]
    i = pl.program_id(0)
    h = _rmsnorm(x_ref[...], g_ref[...]).astype(BF16)

    c0 = 3 * D_ATTN
    gate_b = jnp.dot(h, w_ref[:, c0:c0 + D_CONV], preferred_element_type=F32)
    gate_c = jnp.dot(h, w_ref[:, c0 + D_CONV:c0 + 2 * D_CONV], preferred_element_type=F32)
    conv_in = jnp.dot(h, w_ref[:, c0 + 2 * D_CONV:c0 + 3 * D_CONV], preferred_element_type=F32)
    pool_in = jnp.dot(h, w_ref[:, c0 + 3 * D_CONV:], preferred_element_type=F32)
    u = gate_c * conv_in

    @pl.when(i % tiles_per_seq == 0)
    def _():
        ubuf[0:HALO, :] = jnp.zeros((HALO, D_CONV), F32)
        pbuf[0:HALO, :] = jnp.zeros((HALO, D_POOL), F32)

    ubuf[HALO:HALO + tm, :] = u
    pbuf[HALO:HALO + tm, :] = pool_in

    conv = cw_ref[CONV_WIDTH - 1:CONV_WIDTH, :] * u
    for k in range(1, CONV_WIDTH):
        conv += cw_ref[CONV_WIDTH - 1 - k:CONV_WIDTH - k, :] * ubuf[HALO - k:HALO - k + tm, :]
    ycp_ref[:, 0:D_CONV] = (gate_b * conv).astype(BF16)

    pos = (i % tiles_per_seq) * tm + lax.broadcasted_iota(jnp.int32, (tm, 1), 0)
    for gidx, w in enumerate(POOL_WINDOWS):
        cols = slice(gidx * POOL_GROUP, (gidx + 1) * POOL_GROUP)
        cur = pool_in[:, cols]
        win = cur
        for k in range(1, w):
            win += pbuf[HALO - k:HALO - k + tm, cols]
        count = jnp.minimum(pos + 1, w).astype(F32)
        pooled = (win / count - cur).astype(BF16)
        y = jnp.dot(pooled, pw_ref[gidx], preferred_element_type=F32) * ps_ref[:, cols]
        ycp_ref[:, D_CONV + gidx * POOL_GROUP:D_CONV + (gidx + 1) * POOL_GROUP] = y.astype(BF16)

    ubuf[0:HALO, :] = ubuf[tm:tm + HALO, :]
    pbuf[0:HALO, :] = pbuf[tm:tm + HALO, :]

    for c in range(3):
        cols = slice(c * D_ATTN, (c + 1) * D_ATTN)
        qkv_ref[:, cols] = jnp.dot(h, w_ref[:, cols], preferred_element_type=F32)


def _inproj(x, layer, g, w_in, conv_w, pool_w, pool_scale, seq):
    t = x.shape[0]
    tm = INPROJ_TM
    const = lambda *shape: pl.BlockSpec((None,) + shape, lambda i: (layer,) + (0,) * len(shape),
                                        pipeline_mode=pl.Buffered(1))
    return pl.pallas_call(
        functools.partial(_inproj_kernel, tiles_per_seq=seq // tm),
        grid=(t // tm,),
        in_specs=[
            pl.BlockSpec((tm, D_MODEL), lambda i: (i, 0)),
            const(1, D_MODEL),
            const(D_MODEL, D_IN),
            const(CONV_WIDTH, D_CONV),
            const(len(POOL_WINDOWS), POOL_GROUP, POOL_GROUP),
            const(1, D_POOL),
        ],
        out_specs=[
            pl.BlockSpec((tm, 3 * D_ATTN), lambda i: (i, 0)),
            pl.BlockSpec((tm, D_CP), lambda i: (i, 0)),
        ],
        out_shape=[
            jax.ShapeDtypeStruct((t, 3 * D_ATTN), F32),
            jax.ShapeDtypeStruct((t, D_CP), BF16),
        ],
        scratch_shapes=[pltpu.VMEM((HALO + tm, D_CONV), F32),
                        pltpu.VMEM((HALO + tm, D_POOL), F32)],
        compiler_params=pltpu.CompilerParams(
            dimension_semantics=("arbitrary",), vmem_limit_bytes=VMEM_LIMIT),
        name="inproj",
    )(x, g[:, None, :], w_in, conv_w, pool_w, pool_scale[:, None, :])


def _attn_kernel(q_ref, kp_ref, kc_ref, vp_ref, vc_ref, o_ref, num_ref, m_ref, l_ref,
                 s_ref, p_ref):
    n = pl.program_id(2)
    head_a = lax.broadcasted_iota(jnp.int32, (BLK, LANES), 1) < HEAD_DIM
    qi = lax.broadcasted_iota(jnp.int32, (HEADS_PER_STEP * BLK, 2 * BLK), 0) % BLK
    kj = lax.broadcasted_iota(jnp.int32, (HEADS_PER_STEP * BLK, 2 * BLK), 1)
    band = jnp.logical_and(kj >= qi, kj <= qi + BLK)
    band_prev = jnp.logical_and(band, jnp.logical_or(kj >= BLK, n > 0))
    dims_nt = (((1,), (1,)), ((), ()))

    def rows(off, size, d):
        return pl.ds(off, size) if d == 1 else pl.ds(off, size, stride=d)

    def keys(prev_ref, cur_ref, d, q_off, from_prev):
        if from_prev:
            lo = prev_ref[rows(q_off + SUPER - BLK * d, BLK, d), :]
            return jnp.concatenate([lo, cur_ref[rows(q_off, BLK, d), :]], axis=0).astype(BF16)
        return cur_ref[rows(q_off - BLK * d, 2 * BLK, d), :].astype(BF16)

    def split_heads(x):
        return jnp.where(head_a, x[:BLK], x[BLK:])

    for pat, d in enumerate(DILATIONS):
        nb = SUPER // (BLK * d)
        combos = [(r + d * BLK * b, b == 0) for r in range(d) for b in range(nb)]

        for c, (q_off, from_prev) in enumerate(combos):
            k = keys(kp_ref, kc_ref, d, q_off, from_prev)
            q = (q_ref[rows(q_off, BLK, d), :] * (HEAD_DIM ** -0.5)).astype(BF16)
            zero = jnp.zeros_like(q)
            q2 = jnp.concatenate([jnp.where(head_a, q, zero), jnp.where(head_a, zero, q)], axis=0)
            s2 = lax.dot_general(q2, k, dims_nt, preferred_element_type=F32)
            s_ref[pat, c] = jnp.where(band_prev if from_prev else band, s2, NEG_INF)

        for c, (q_off, _) in enumerate(combos):
            s = s_ref[pat, c]
            m = jnp.max(jnp.maximum(s[:, :BLK], s[:, BLK:]), axis=1, keepdims=True)
            p = jnp.exp(s - m)
            l = jnp.sum(p[:, :BLK] + p[:, BLK:], axis=1, keepdims=True)
            p_ref[pat, c] = p.astype(BF16)
            qr = rows(q_off, BLK, d)
            m_ref[pat, qr, :] = split_heads(jnp.broadcast_to(m, (HEADS_PER_STEP * BLK, LANES)))
            l_ref[pat, qr, :] = split_heads(jnp.broadcast_to(l, (HEADS_PER_STEP * BLK, LANES)))

        for c, (q_off, from_prev) in enumerate(combos):
            v = keys(vp_ref, vc_ref, d, q_off, from_prev)
            pv2 = jnp.dot(p_ref[pat, c], v, preferred_element_type=F32)
            num_ref[pat, rows(q_off, BLK, d), :] = split_heads(pv2)

    m_all = m_ref[...]
    w = jnp.exp(m_all - jnp.max(m_all, axis=0)[None])
    den = jnp.sum(w * l_ref[...], axis=0)
    o_ref[...] = (jnp.sum(w * num_ref[...], axis=0) / den).astype(o_ref.dtype)


def _attn(qkv, batch, seq):
    ns = seq // SUPER
    hg = D_ATTN // LANES
    blk = lambda imap: pl.BlockSpec((None, SUPER, LANES), imap)
    return pl.pallas_call(
        _attn_kernel,
        grid=(batch, hg, ns),
        in_specs=[
            blk(lambda b, g, n: (b, n, g)),
            blk(lambda b, g, n: (b, jnp.maximum(n - 1, 0), hg + g)),
            blk(lambda b, g, n: (b, n, hg + g)),
            blk(lambda b, g, n: (b, jnp.maximum(n - 1, 0), 2 * hg + g)),
            blk(lambda b, g, n: (b, n, 2 * hg + g)),
        ],
        out_specs=blk(lambda b, g, n: (b, n, g)),
        out_shape=jax.ShapeDtypeStruct((batch, seq, D_ATTN), BF16),
        scratch_shapes=[pltpu.VMEM((len(DILATIONS), SUPER, LANES), F32)] * 3 + [
            pltpu.VMEM((len(DILATIONS), SUPER // BLK, HEADS_PER_STEP * BLK, 2 * BLK), F32),
            pltpu.VMEM((len(DILATIONS), SUPER // BLK, HEADS_PER_STEP * BLK, 2 * BLK), BF16)],
        compiler_params=pltpu.CompilerParams(
            dimension_semantics=("parallel", "parallel", "arbitrary"),
            vmem_limit_bytes=VMEM_LIMIT),
        name="attn",
    )(qkv, qkv, qkv, qkv, qkv)


def _outproj_kernel(x_ref, a_ref, cp_ref, w_ref, o_ref):
    o_ref[...] = (x_ref[...]
                  + jnp.dot(a_ref[...], w_ref[0:D_ATTN, :], preferred_element_type=F32)
                  + jnp.dot(cp_ref[...], w_ref[D_ATTN:, :], preferred_element_type=F32))


def _outproj(x, y_attn, y_cp, layer, w_out):
    t = x.shape[0]
    tm = OUTPROJ_TM
    return pl.pallas_call(
        _outproj_kernel,
        grid=(t // tm,),
        in_specs=[
            pl.BlockSpec((tm, D_MODEL), lambda i: (i, 0)),
            pl.BlockSpec((tm, D_ATTN), lambda i: (i, 0)),
            pl.BlockSpec((tm, D_CP), lambda i: (i, 0)),
            pl.BlockSpec((None, D_MODEL, D_MODEL), lambda i: (layer, 0, 0),
                         pipeline_mode=pl.Buffered(1)),
        ],
        out_specs=pl.BlockSpec((tm, D_MODEL), lambda i: (i, 0)),
        out_shape=jax.ShapeDtypeStruct((t, D_MODEL), F32),
        compiler_params=pltpu.CompilerParams(
            dimension_semantics=("parallel",), vmem_limit_bytes=VMEM_LIMIT),
        name="outproj",
    )(x, y_attn, y_cp, w_out)


def kernel(x, ffn1_norm, ffn1_w_gate, ffn1_w_up, ffn1_w_down, mix_norm, w_in, conv_w, pool_w, pool_scale, w_out, ffn2_norm, ffn2_w_gate, ffn2_w_up, ffn2_w_down, final_norm):
    batch, seq, _ = x.shape
    depth = w_in.shape[0]
    assert seq % SUPER == 0 and seq % INPROJ_TM == 0 and (batch * seq) % FFN_TM == 0
    t = batch * seq
    x = x.reshape(t, D_MODEL)
    ffn1 = (ffn1_norm, ffn1_w_gate.astype(BF16), ffn1_w_up.astype(BF16), ffn1_w_down.astype(BF16))
    ffn2 = (ffn2_norm, ffn2_w_gate.astype(BF16), ffn2_w_up.astype(BF16), ffn2_w_down.astype(BF16))
    w_in, pool_w, w_out = w_in.astype(BF16), pool_w.astype(BF16), w_out.astype(BF16)
    for l in range(depth):
        x = _ffn(x, l, *ffn1)
        qkv, y_cp = _inproj(x, l, mix_norm, w_in, conv_w, pool_w, pool_scale, seq)
        y_attn = _attn(qkv.reshape(batch, seq, 3 * D_ATTN), batch, seq)
        x = _outproj(x, y_attn.reshape(t, D_ATTN), y_cp, l, w_out)
        x = _ffn(x, l, *ffn2, final_g=final_norm if l == depth - 1 else None)
    return x.reshape(batch, seq, D_MODEL)
```

```python
import functools

import jax
import jax.numpy as jnp
from jax import lax
from jax.experimental import pallas as pl
from jax.experimental.pallas import tpu as pltpu

D_MODEL = 2048
D_FF = 5632
D_ATTN = 1024
D_CONV = 512
D_POOL = 512
D_IN = 3 * D_ATTN + 3 * D_CONV + D_POOL
D_CP = D_CONV + D_POOL
HEAD_DIM = 64
CONV_WIDTH = 3
POOL_WINDOWS = (2, 4, 8, 16)
POOL_GROUP = D_POOL // len(POOL_WINDOWS)
DILATIONS = (1, 4, 16)
BLK = 128
RMS_EPS = 1e-6
NEG_INF = -1e30
FFN_RESIDUAL = 0.5

LANES = 128
HALO = 16
SUPER = BLK * max(DILATIONS)
HEADS_PER_STEP = LANES // HEAD_DIM

FFN_TM, FFN_TF = 1024, 512
INPROJ_TM = 512
OUTPROJ_TM = 1024
VMEM_LIMIT = 60 * 1024 * 1024

F32 = jnp.float32
BF16 = jnp.bfloat16


def _rmsnorm(x, g):
    return x * lax.rsqrt(jnp.mean(x * x, axis=-1, keepdims=True) + RMS_EPS) * g


def _ffn_kernel(*refs, final):
    if final:
        x_ref, g_ref, wg_ref, wu_ref, wd_ref, fg_ref, o_ref, h_ref = refs
    else:
        x_ref, g_ref, wg_ref, wu_ref, wd_ref, o_ref, h_ref = refs
    j = pl.program_id(1)

    @pl.when(j == 0)
    def _():
        x = x_ref[...]
        h_ref[...] = _rmsnorm(x, g_ref[...]).astype(BF16)
        o_ref[...] = x

    h = h_ref[...]
    gate = jnp.dot(h, wg_ref[...], preferred_element_type=F32)
    up = jnp.dot(h, wu_ref[...], preferred_element_type=F32)
    a = (gate * jax.nn.sigmoid(gate) * up * FFN_RESIDUAL).astype(BF16)
    o_ref[...] += jnp.dot(a, wd_ref[...], preferred_element_type=F32)

    if final:
        @pl.when(j == pl.num_programs(1) - 1)
        def _():
            o_ref[...] = _rmsnorm(o_ref[...], fg_ref[...])


def _ffn(x, layer, g, wg, wu, wd, final_g=None):
    t = x.shape[0]
    final = final_g is not None
    in_specs = [
        pl.BlockSpec((FFN_TM, D_MODEL), lambda i, j: (i, 0), pipeline_mode=pl.Buffered(1)),
        pl.BlockSpec((None, 1, D_MODEL), lambda i, j: (layer, 0, 0)),
        pl.BlockSpec((None, D_MODEL, FFN_TF), lambda i, j: (layer, 0, j)),
        pl.BlockSpec((None, D_MODEL, FFN_TF), lambda i, j: (layer, 0, j)),
        pl.BlockSpec((None, FFN_TF, D_MODEL), lambda i, j: (layer, j, 0)),
    ]
    args = [x, g[:, None, :], wg, wu, wd]
    if final:
        in_specs.append(pl.BlockSpec((1, D_MODEL), lambda i, j: (0, 0)))
        args.append(final_g.reshape(1, D_MODEL))
    return pl.pallas_call(
        functools.partial(_ffn_kernel, final=final),
        grid=(t // FFN_TM, D_FF // FFN_TF),
        in_specs=in_specs,
        out_specs=pl.BlockSpec((FFN_TM, D_MODEL), lambda i, j: (i, 0)),
        out_shape=jax.ShapeDtypeStruct((t, D_MODEL), F32),
        scratch_shapes=[pltpu.VMEM((FFN_TM, D_MODEL), BF16)],
        compiler_params=pltpu.CompilerParams(
            dimension_semantics=("parallel", "arbitrary"), vmem_limit_bytes=VMEM_LIMIT),
        name="ffn_final" if final else "ffn",
    )(*args)


def _inproj_kernel(x_ref, g_ref, w_ref, cw_ref, pw_ref, ps_ref, qkv_ref, ycp_ref,
                   ubuf, pbuf, *, tiles_per_seq):
    tm = x_ref.shape[0]
    i = pl.program_id(0)

    @pl.when(i % tiles_per_seq == 0)
    def _():
        ubuf[0:HALO, :] = jnp.zeros((HALO, D_CONV), F32)
        pbuf[0:HALO, :] = jnp.zeros((HALO, D_POOL), F32)

    h = _rmsnorm(x_ref[...], g_ref[...]).astype(BF16)

    c0 = 3 * D_ATTN
    gate_b = jnp.dot(h, w_ref[:, c0:c0 + D_CONV], preferred_element_type=F32)
    gate_c = jnp.dot(h, w_ref[:, c0 + D_CONV:c0 + 2 * D_CONV], preferred_element_type=F32)
    conv_in = jnp.dot(h, w_ref[:, c0 + 2 * D_CONV:c0 + 3 * D_CONV], preferred_element_type=F32)
    pool_in = jnp.dot(h, w_ref[:, c0 + 3 * D_CONV:], preferred_element_type=F32)
    u = gate_c * conv_in
    for c in range(3):
        cols = slice(c * D_ATTN, (c + 1) * D_ATTN)
        qkv_ref[:, cols] = jnp.dot(h, w_ref[:, cols], preferred_element_type=F32)

    ubuf[HALO:HALO + tm, :] = u
    pbuf[HALO:HALO + tm, :] = pool_in

    conv = cw_ref[CONV_WIDTH - 1:CONV_WIDTH, :] * u
    for k in range(1, CONV_WIDTH):
        conv += cw_ref[CONV_WIDTH - 1 - k:CONV_WIDTH - k, :] * ubuf[HALO - k:HALO - k + tm, :]
    ycp_ref[:, 0:D_CONV] = (gate_b * conv).astype(BF16)

    pos = (i % tiles_per_seq) * tm + lax.broadcasted_iota(jnp.int32, (tm, 1), 0)
    for gidx, w in enumerate(POOL_WINDOWS):
        cols = slice(gidx * POOL_GROUP, (gidx + 1) * POOL_GROUP)
        cur = pool_in[:, cols]
        win = cur
        for k in range(1, w):
            win += pbuf[HALO - k:HALO - k + tm, cols]
        count = jnp.minimum(pos + 1, w).astype(F32)
        pooled = (win / count - cur).astype(BF16)
        y = jnp.dot(pooled, pw_ref[gidx], preferred_element_type=F32) * ps_ref[:, cols]
        ycp_ref[:, D_CONV + gidx * POOL_GROUP:D_CONV + (gidx + 1) * POOL_GROUP] = y.astype(BF16)

    ubuf[0:HALO, :] = ubuf[tm:tm + HALO, :]
    pbuf[0:HALO, :] = pbuf[tm:tm + HALO, :]


def _inproj(x, layer, g, w_in, conv_w, pool_w, pool_scale, seq):
    t = x.shape[0]
    tm = INPROJ_TM
    const = lambda *shape: pl.BlockSpec((None,) + shape, lambda i: (layer,) + (0,) * len(shape),
                                        pipeline_mode=pl.Buffered(1))
    return pl.pallas_call(
        functools.partial(_inproj_kernel, tiles_per_seq=seq // tm),
        grid=(t // tm,),
        in_specs=[
            pl.BlockSpec((tm, D_MODEL), lambda i: (i, 0)),
            const(1, D_MODEL),
            const(D_MODEL, D_IN),
            const(CONV_WIDTH, D_CONV),
            const(len(POOL_WINDOWS), POOL_GROUP, POOL_GROUP),
            const(1, D_POOL),
        ],
        out_specs=[
            pl.BlockSpec((tm, 3 * D_ATTN), lambda i: (i, 0)),
            pl.BlockSpec((tm, D_CP), lambda i: (i, 0)),
        ],
        out_shape=[
            jax.ShapeDtypeStruct((t, 3 * D_ATTN), F32),
            jax.ShapeDtypeStruct((t, D_CP), BF16),
        ],
        scratch_shapes=[pltpu.VMEM((HALO + tm, D_CONV), F32),
                        pltpu.VMEM((HALO + tm, D_POOL), F32)],
        compiler_params=pltpu.CompilerParams(
            dimension_semantics=("arbitrary",), vmem_limit_bytes=VMEM_LIMIT),
        name="inproj",
    )(x, g[:, None, :], w_in, conv_w, pool_w, pool_scale[:, None, :])


def _attn_kernel(q_ref, kp_ref, kc_ref, vp_ref, vc_ref, o_ref, num_ref, m_ref, l_ref,
                 s_ref, bias_ref):
    n = pl.program_id(2)
    head_a = lax.broadcasted_iota(jnp.int32, (BLK, LANES), 1) < HEAD_DIM
    qi = lax.broadcasted_iota(jnp.int32, (HEADS_PER_STEP * BLK, 2 * BLK), 0) % BLK
    kj = lax.broadcasted_iota(jnp.int32, (HEADS_PER_STEP * BLK, 2 * BLK), 1)
    band = jnp.logical_and(kj >= qi, kj <= qi + BLK)
    band_prev = jnp.logical_and(band, jnp.logical_or(kj >= BLK, n > 0))
    bias_ref[0] = jnp.where(band, 0.0, NEG_INF)
    bias_ref[1] = jnp.where(band_prev, 0.0, NEG_INF)
    dims_nt = (((1,), (1,)), ((), ()))

    def rows(off, size, d):
        return pl.ds(off, size) if d == 1 else pl.ds(off, size, stride=d)

    def keys(prev_ref, cur_ref, d, q_off, from_prev):
        if from_prev:
            lo = prev_ref[rows(q_off + SUPER - BLK * d, BLK, d), :]
            return jnp.concatenate([lo, cur_ref[rows(q_off, BLK, d), :]], axis=0).astype(BF16)
        return cur_ref[rows(q_off - BLK * d, 2 * BLK, d), :].astype(BF16)

    def split_heads(x):
        return jnp.where(head_a, x[:BLK], x[BLK:])

    for pat, d in enumerate(DILATIONS):
        nb = SUPER // (BLK * d)
        combos = [(r + d * BLK * b, b == 0) for r in range(d) for b in range(nb)]

        for c, (q_off, from_prev) in enumerate(combos):
            k = keys(kp_ref, kc_ref, d, q_off, from_prev)
            q = (q_ref[rows(q_off, BLK, d), :] * (HEAD_DIM ** -0.5)).astype(BF16)
            zero = jnp.zeros_like(q)
            q2 = jnp.concatenate([jnp.where(head_a, q, zero), jnp.where(head_a, zero, q)], axis=0)
            s_ref[pat, c] = lax.dot_general(q2, k, dims_nt, preferred_element_type=F32)

        for c, (q_off, from_prev) in enumerate(combos):
            s = s_ref[pat, c] + bias_ref[1 if from_prev else 0]
            m = jnp.max(jnp.maximum(s[:, :BLK], s[:, BLK:]), axis=1, keepdims=True)
            p = jnp.exp(s - m)
            l = jnp.sum(p[:, :BLK] + p[:, BLK:], axis=1, keepdims=True)
            qr = rows(q_off, BLK, d)
            m_ref[pat, qr, :] = split_heads(jnp.broadcast_to(m, (HEADS_PER_STEP * BLK, LANES)))
            l_ref[pat, qr, :] = split_heads(jnp.broadcast_to(l, (HEADS_PER_STEP * BLK, LANES)))
            v = keys(vp_ref, vc_ref, d, q_off, from_prev)
            pv2 = jnp.dot(p.astype(BF16), v, preferred_element_type=F32)
            num_ref[pat, qr, :] = split_heads(pv2)

    m_all = m_ref[...]
    w = jnp.exp(m_all - jnp.max(m_all, axis=0)[None])
    den = jnp.sum(w * l_ref[...], axis=0)
    o_ref[...] = (jnp.sum(w * num_ref[...], axis=0) / den).astype(o_ref.dtype)


def _attn(qkv, batch, seq):
    ns = seq // SUPER
    hg = D_ATTN // LANES
    blk = lambda imap: pl.BlockSpec((None, SUPER, LANES), imap)
    return pl.pallas_call(
        _attn_kernel,
        grid=(batch, hg, ns),
        in_specs=[
            blk(lambda b, g, n: (b, n, g)),
            blk(lambda b, g, n: (b, jnp.maximum(n - 1, 0), hg + g)),
            blk(lambda b, g, n: (b, n, hg + g)),
            blk(lambda b, g, n: (b, jnp.maximum(n - 1, 0), 2 * hg + g)),
            blk(lambda b, g, n: (b, n, 2 * hg + g)),
        ],
        out_specs=blk(lambda b, g, n: (b, n, g)),
        out_shape=jax.ShapeDtypeStruct((batch, seq, D_ATTN), BF16),
        scratch_shapes=[pltpu.VMEM((len(DILATIONS), SUPER, LANES), F32)] * 3 + [
            pltpu.VMEM((len(DILATIONS), SUPER // BLK, HEADS_PER_STEP * BLK, 2 * BLK), F32),
            pltpu.VMEM((2, HEADS_PER_STEP * BLK, 2 * BLK), F32)],
        compiler_params=pltpu.CompilerParams(
            dimension_semantics=("parallel", "parallel", "arbitrary"),
            vmem_limit_bytes=VMEM_LIMIT),
        name="attn",
    )(qkv, qkv, qkv, qkv, qkv)


def _outproj_kernel(x_ref, a_ref, cp_ref, w_ref, o_ref):
    o_ref[...] = (x_ref[...]
                  + jnp.dot(a_ref[...], w_ref[0:D_ATTN, :], preferred_element_type=F32)
                  + jnp.dot(cp_ref[...], w_ref[D_ATTN:, :], preferred_element_type=F32))


def _outproj(x, y_attn, y_cp, layer, w_out):
    t = x.shape[0]
    tm = OUTPROJ_TM
    return pl.pallas_call(
        _outproj_kernel,
        grid=(t // tm,),
        in_specs=[
            pl.BlockSpec((tm, D_MODEL), lambda i: (i, 0)),
            pl.BlockSpec((tm, D_ATTN), lambda i: (i, 0)),
            pl.BlockSpec((tm, D_CP), lambda i: (i, 0)),
            pl.BlockSpec((None, D_MODEL, D_MODEL), lambda i: (layer, 0, 0),
                         pipeline_mode=pl.Buffered(1)),
        ],
        out_specs=pl.BlockSpec((tm, D_MODEL), lambda i: (i, 0)),
        out_shape=jax.ShapeDtypeStruct((t, D_MODEL), F32),
        compiler_params=pltpu.CompilerParams(
            dimension_semantics=("parallel",), vmem_limit_bytes=VMEM_LIMIT),
        name="outproj",
    )(x, y_attn, y_cp, w_out)


def kernel(x, ffn1_norm, ffn1_w_gate, ffn1_w_up, ffn1_w_down, mix_norm, w_in, conv_w, pool_w, pool_scale, w_out, ffn2_norm, ffn2_w_gate, ffn2_w_up, ffn2_w_down, final_norm):
    batch, seq, _ = x.shape
    depth = w_in.shape[0]
    assert seq % SUPER == 0 and seq % INPROJ_TM == 0 and (batch * seq) % FFN_TM == 0
    t = batch * seq
    x = x.reshape(t, D_MODEL)
    ffn1 = (ffn1_norm, ffn1_w_gate.astype(BF16), ffn1_w_up.astype(BF16), ffn1_w_down.astype(BF16))
    ffn2 = (ffn2_norm, ffn2_w_gate.astype(BF16), ffn2_w_up.astype(BF16), ffn2_w_down.astype(BF16))
    w_in, pool_w, w_out = w_in.astype(BF16), pool_w.astype(BF16), w_out.astype(BF16)
    for l in range(depth):
        x = _ffn(x, l, *ffn1)
        qkv, y_cp = _inproj(x, l, mix_norm, w_in, conv_w, pool_w, pool_scale, seq)
        y_attn = _attn(qkv.reshape(batch, seq, 3 * D_ATTN), batch, seq)
        x = _outproj(x, y_attn.reshape(t, D_ATTN), y_cp, l, w_out)
        x = _ffn(x, l, *ffn2, final_g=final_norm if l == depth - 1 else None)
    return x.reshape(batch, seq, D_MODEL)
```

```python
import functools

import jax
import jax.numpy as jnp
from jax import lax
from jax.experimental import pallas as pl
from jax.experimental.pallas import tpu as pltpu

D_MODEL = 2048
D_FF = 5632
D_ATTN = 1024
D_CONV = 512
D_POOL = 512
D_IN = 3 * D_ATTN + 3 * D_CONV + D_POOL
D_CP = D_CONV + D_POOL
HEAD_DIM = 64
CONV_WIDTH = 3
POOL_WINDOWS = (2, 4, 8, 16)
POOL_GROUP = D_POOL // len(POOL_WINDOWS)
DILATIONS = (1, 4, 16)
BLK = 128
RMS_EPS = 1e-6
NEG_INF = -1e30
FFN_RESIDUAL = 0.5

LANES = 128
HALO = 16
SUPER = BLK * max(DILATIONS)
HEADS_PER_STEP = LANES // HEAD_DIM

FFN_TM, FFN_TF = 1024, 512
FFN_CHUNK = 256
INPROJ_TM = 512
OUTPROJ_TM = 1024
VMEM_LIMIT = 60 * 1024 * 1024

F32 = jnp.float32
BF16 = jnp.bfloat16


def _rmsnorm(x, g):
    return x * lax.rsqrt(jnp.mean(x * x, axis=-1, keepdims=True) + RMS_EPS) * g


def _ffn_kernel(*refs, final):
    if final:
        x_ref, g_ref, wg_ref, wu_ref, wd_ref, fg_ref, o_ref, h_ref = refs
    else:
        x_ref, g_ref, wg_ref, wu_ref, wd_ref, o_ref, h_ref = refs
    j = pl.program_id(1)

    @pl.when(j == 0)
    def _():
        x = x_ref[...]
        h_ref[...] = _rmsnorm(x, g_ref[...]).astype(BF16)
        o_ref[...] = x

    h = h_ref[...]
    pieces = [slice(c, c + FFN_CHUNK) for c in range(0, FFN_TF, FFN_CHUNK)]
    gate_up = [(jnp.dot(h, wg_ref[:, cs], preferred_element_type=F32),
                jnp.dot(h, wu_ref[:, cs], preferred_element_type=F32)) for cs in pieces]
    down = None
    for cs, (gate, up) in zip(pieces, gate_up):
        a = (gate * jax.nn.sigmoid(gate) * up * FFN_RESIDUAL).astype(BF16)
        d = jnp.dot(a, wd_ref[cs, :], preferred_element_type=F32)
        down = d if down is None else down + d
    o_ref[...] += down

    if final:
        @pl.when(j == pl.num_programs(1) - 1)
        def _():
            o_ref[...] = _rmsnorm(o_ref[...], fg_ref[...])


def _ffn(x, layer, g, wg, wu, wd, final_g=None):
    t = x.shape[0]
    final = final_g is not None
    in_specs = [
        pl.BlockSpec((FFN_TM, D_MODEL), lambda i, j: (i, 0), pipeline_mode=pl.Buffered(1)),
        pl.BlockSpec((None, 1, D_MODEL), lambda i, j: (layer, 0, 0)),
        pl.BlockSpec((None, D_MODEL, FFN_TF), lambda i, j: (layer, 0, j)),
        pl.BlockSpec((None, D_MODEL, FFN_TF), lambda i, j: (layer, 0, j)),
        pl.BlockSpec((None, FFN_TF, D_MODEL), lambda i, j: (layer, j, 0)),
    ]
    args = [x, g[:, None, :], wg, wu, wd]
    if final:
        in_specs.append(pl.BlockSpec((1, D_MODEL), lambda i, j: (0, 0)))
        args.append(final_g.reshape(1, D_MODEL))
    return pl.pallas_call(
        functools.partial(_ffn_kernel, final=final),
        grid=(t // FFN_TM, D_FF // FFN_TF),
        in_specs=in_specs,
        out_specs=pl.BlockSpec((FFN_TM, D_MODEL), lambda i, j: (i, 0)),
        out_shape=jax.ShapeDtypeStruct((t, D_MODEL), F32),
        scratch_shapes=[pltpu.VMEM((FFN_TM, D_MODEL), BF16)],
        compiler_params=pltpu.CompilerParams(
            dimension_semantics=("parallel", "arbitrary"), vmem_limit_bytes=VMEM_LIMIT),
        name="ffn_final" if final else "ffn",
    )(*args)


def _inproj_kernel(x_ref, g_ref, w_ref, cw_ref, pw_ref, ps_ref, qkv_ref, ycp_ref,
                   ubuf, pbuf, *, tiles_per_seq):
    tm = x_ref.shape[0]
    i = pl.program_id(0)

    @pl.when(i % tiles_per_seq == 0)
    def _():
        ubuf[0:HALO, :] = jnp.zeros((HALO, D_CONV), F32)
        pbuf[0:HALO, :] = jnp.zeros((HALO, D_POOL), F32)

    h = _rmsnorm(x_ref[...], g_ref[...]).astype(BF16)

    c0 = 3 * D_ATTN
    gate_b = jnp.dot(h, w_ref[:, c0:c0 + D_CONV], preferred_element_type=F32)
    gate_c = jnp.dot(h, w_ref[:, c0 + D_CONV:c0 + 2 * D_CONV], preferred_element_type=F32)
    conv_in = jnp.dot(h, w_ref[:, c0 + 2 * D_CONV:c0 + 3 * D_CONV], preferred_element_type=F32)
    pool_in = jnp.dot(h, w_ref[:, c0 + 3 * D_CONV:], preferred_element_type=F32)
    u = gate_c * conv_in
    for c in range(3):
        cols = slice(c * D_ATTN, (c + 1) * D_ATTN)
        qkv_ref[:, cols] = jnp.dot(h, w_ref[:, cols], preferred_element_type=F32)

    ubuf[HALO:HALO + tm, :] = u
    pbuf[HALO:HALO + tm, :] = pool_in

    conv = cw_ref[CONV_WIDTH - 1:CONV_WIDTH, :] * u
    for k in range(1, CONV_WIDTH):
        conv += cw_ref[CONV_WIDTH - 1 - k:CONV_WIDTH - k, :] * ubuf[HALO - k:HALO - k + tm, :]
    ycp_ref[:, 0:D_CONV] = (gate_b * conv).astype(BF16)

    pos = (i % tiles_per_seq) * tm + lax.broadcasted_iota(jnp.int32, (tm, 1), 0)
    for gidx, w in enumerate(POOL_WINDOWS):
        cols = slice(gidx * POOL_GROUP, (gidx + 1) * POOL_GROUP)
        cur = pool_in[:, cols]
        win = cur
        for k in range(1, w):
            win += pbuf[HALO - k:HALO - k + tm, cols]
        count = jnp.minimum(pos + 1, w).astype(F32)
        pooled = (win / count - cur).astype(BF16)
        y = jnp.dot(pooled, pw_ref[gidx], preferred_element_type=F32) * ps_ref[:, cols]
        ycp_ref[:, D_CONV + gidx * POOL_GROUP:D_CONV + (gidx + 1) * POOL_GROUP] = y.astype(BF16)

    ubuf[0:HALO, :] = ubuf[tm:tm + HALO, :]
    pbuf[0:HALO, :] = pbuf[tm:tm + HALO, :]


def _inproj(x, layer, g, w_in, conv_w, pool_w, pool_scale, seq):
    t = x.shape[0]
    tm = INPROJ_TM
    const = lambda *shape: pl.BlockSpec((None,) + shape, lambda i: (layer,) + (0,) * len(shape),
                                        pipeline_mode=pl.Buffered(1))
    return pl.pallas_call(
        functools.partial(_inproj_kernel, tiles_per_seq=seq // tm),
        grid=(t // tm,),
        in_specs=[
            pl.BlockSpec((tm, D_MODEL), lambda i: (i, 0)),
            const(1, D_MODEL),
            const(D_MODEL, D_IN),
            const(CONV_WIDTH, D_CONV),
            const(len(POOL_WINDOWS), POOL_GROUP, POOL_GROUP),
            const(1, D_POOL),
        ],
        out_specs=[
            pl.BlockSpec((tm, 3 * D_ATTN), lambda i: (i, 0)),
            pl.BlockSpec((tm, D_CP), lambda i: (i, 0)),
        ],
        out_shape=[
            jax.ShapeDtypeStruct((t, 3 * D_ATTN), F32),
            jax.ShapeDtypeStruct((t, D_CP), BF16),
        ],
        scratch_shapes=[pltpu.VMEM((HALO + tm, D_CONV), F32),
                        pltpu.VMEM((HALO + tm, D_POOL), F32)],
        compiler_params=pltpu.CompilerParams(
            dimension_semantics=("arbitrary",), vmem_limit_bytes=VMEM_LIMIT),
        name="inproj",
    )(x, g[:, None, :], w_in, conv_w, pool_w, pool_scale[:, None, :])


def _attn_kernel(q_ref, kp_ref, kc_ref, vp_ref, vc_ref, o_ref, stat1_ref, stat4_ref,
                 q4_ref, k4_ref, v4_ref, s_ref, bias_ref, o4_ref):
    n = pl.program_id(2)
    quarter = SUPER // 4
    head_a = lax.broadcasted_iota(jnp.int32, (BLK, LANES), 1) < HEAD_DIM
    qi = lax.broadcasted_iota(jnp.int32, (HEADS_PER_STEP * BLK, 2 * BLK), 0) % BLK
    kj = lax.broadcasted_iota(jnp.int32, (HEADS_PER_STEP * BLK, 2 * BLK), 1)
    band = jnp.logical_and(kj >= qi, kj <= qi + BLK)
    band_prev = jnp.logical_and(band, jnp.logical_or(kj >= BLK, n > 0))
    bias_ref[0] = jnp.where(band, 0.0, NEG_INF)
    bias_ref[1] = jnp.where(band_prev, 0.0, NEG_INF)
    dims_nt = (((1,), (1,)), ((), ()))

    for r4 in range(4):
        by4 = pl.ds(r4, quarter, stride=4)
        q4_ref[r4] = q_ref[by4, :]
        k4_ref[r4, 0:quarter] = kp_ref[by4, :]
        k4_ref[r4, quarter:2 * quarter] = kc_ref[by4, :]
        v4_ref[r4, 0:quarter] = vp_ref[by4, :]
        v4_ref[r4, quarter:2 * quarter] = vc_ref[by4, :]

    def natural_kv(prev_ref, cur_ref, b):
        if b == 0:
            lo = prev_ref[SUPER - BLK:SUPER, :]
            return jnp.concatenate([lo, cur_ref[0:BLK, :]], axis=0)
        return cur_ref[BLK * (b - 1):BLK * (b + 1), :]

    blocks = [[], [], []]
    for b in range(SUPER // BLK):
        rows = slice(BLK * b, BLK * (b + 1))
        blocks[0].append((lambda rows=rows: q_ref[rows, :],
                          lambda b=b: natural_kv(kp_ref, kc_ref, b),
                          lambda b=b: natural_kv(vp_ref, vc_ref, b),
                          b == 0, (stat1_ref, (rows,))))
    for r4 in range(4):
        for b in range(quarter // BLK):
            rows = slice(BLK * b, BLK * (b + 1))
            krows = slice(quarter + BLK * (b - 1), quarter + BLK * (b + 1))
            blocks[1].append((lambda r4=r4, rows=rows: q4_ref[r4, rows, :],
                              lambda r4=r4, krows=krows: k4_ref[r4, krows, :],
                              lambda r4=r4, krows=krows: v4_ref[r4, krows, :],
                              b == 0, (stat4_ref, (0, r4, rows))))
    for r16 in range(16):
        r4, sub = r16 % 4, r16 // 4
        rows = pl.ds(sub, BLK, stride=4)
        krows = pl.ds(sub, 2 * BLK, stride=4)
        blocks[2].append((lambda r4=r4, rows=rows: q4_ref[r4, rows, :],
                          lambda r4=r4, krows=krows: k4_ref[r4, krows, :],
                          lambda r4=r4, krows=krows: v4_ref[r4, krows, :],
                          True, (stat4_ref, (1, r4, rows))))

    def split_heads(x):
        return jnp.where(head_a, x[:BLK], x[BLK:])

    for pat, pattern_blocks in enumerate(blocks):
        for c, (load_q, load_k, _, _, _) in enumerate(pattern_blocks):
            q = (load_q() * (HEAD_DIM ** -0.5)).astype(BF16)
            zero = jnp.zeros_like(q)
            q2 = jnp.concatenate([jnp.where(head_a, q, zero), jnp.where(head_a, zero, q)], axis=0)
            s_ref[pat, c] = lax.dot_general(q2, load_k().astype(BF16), dims_nt,
                                            preferred_element_type=F32)

        for c, (_, _, load_v, first, (stat_ref, where)) in enumerate(pattern_blocks):
            s = s_ref[pat, c] + bias_ref[1 if first else 0]
            m = jnp.max(jnp.maximum(s[:, :BLK], s[:, BLK:]), axis=1, keepdims=True)
            p = jnp.exp(s - m)
            l = jnp.sum(p[:, :BLK] + p[:, BLK:], axis=1, keepdims=True)
            pv2 = jnp.dot(p.astype(BF16), load_v().astype(BF16), preferred_element_type=F32)
            stat_ref[(0,) + where] = split_heads(pv2)
            stat_ref[(1,) + where] = split_heads(jnp.broadcast_to(m, (HEADS_PER_STEP * BLK, LANES)))
            stat_ref[(2,) + where] = split_heads(jnp.broadcast_to(l, (HEADS_PER_STEP * BLK, LANES)))

    for r4 in range(4):
        by4 = pl.ds(r4, quarter, stride=4)
        num, mx, den = zip((stat1_ref[0, by4, :], stat1_ref[1, by4, :], stat1_ref[2, by4, :]),
                           *[(stat4_ref[0, pat, r4], stat4_ref[1, pat, r4], stat4_ref[2, pat, r4])
                             for pat in range(2)])
        m_max = functools.reduce(jnp.maximum, mx)
        w = [jnp.exp(m - m_max) for m in mx]
        o4_ref[by4, :] = (sum(wi * ni for wi, ni in zip(w, num))
                          / sum(wi * di for wi, di in zip(w, den)))
    o_ref[...] = o4_ref[...].astype(o_ref.dtype)


def _attn(qkv, batch, seq):
    ns = seq // SUPER
    hg = D_ATTN // LANES
    blk = lambda imap: pl.BlockSpec((None, SUPER, LANES), imap)
    return pl.pallas_call(
        _attn_kernel,
        grid=(batch, hg, ns),
        in_specs=[
            blk(lambda b, g, n: (b, n, g)),
            blk(lambda b, g, n: (b, jnp.maximum(n - 1, 0), hg + g)),
            blk(lambda b, g, n: (b, n, hg + g)),
            blk(lambda b, g, n: (b, jnp.maximum(n - 1, 0), 2 * hg + g)),
            blk(lambda b, g, n: (b, n, 2 * hg + g)),
        ],
        out_specs=blk(lambda b, g, n: (b, n, g)),
        out_shape=jax.ShapeDtypeStruct((batch, seq, D_ATTN), BF16),
        scratch_shapes=[
            pltpu.VMEM((3, SUPER, LANES), F32),
            pltpu.VMEM((3, 2, 4, SUPER // 4, LANES), F32),
            pltpu.VMEM((4, SUPER // 4, LANES), F32),
            pltpu.VMEM((4, 2 * SUPER // 4, LANES), F32),
            pltpu.VMEM((4, 2 * SUPER // 4, LANES), F32),
            pltpu.VMEM((len(DILATIONS), SUPER // BLK, HEADS_PER_STEP * BLK, 2 * BLK), F32),
            pltpu.VMEM((2, HEADS_PER_STEP * BLK, 2 * BLK), F32),
            pltpu.VMEM((SUPER, LANES), F32)],
        compiler_params=pltpu.CompilerParams(
            dimension_semantics=("parallel", "parallel", "arbitrary"),
            vmem_limit_bytes=VMEM_LIMIT),
        name="attn",
    )(qkv, qkv, qkv, qkv, qkv)


def _outproj_kernel(x_ref, a_ref, cp_ref, w_ref, o_ref):
    o_ref[...] = (x_ref[...]
                  + jnp.dot(a_ref[...], w_ref[0:D_ATTN, :], preferred_element_type=F32)
                  + jnp.dot(cp_ref[...], w_ref[D_ATTN:, :], preferred_element_type=F32))


def _outproj(x, y_attn, y_cp, layer, w_out):
    t = x.shape[0]
    tm = OUTPROJ_TM
    return pl.pallas_call(
        _outproj_kernel,
        grid=(t // tm,),
        in_specs=[
            pl.BlockSpec((tm, D_MODEL), lambda i: (i, 0)),
            pl.BlockSpec((tm, D_ATTN), lambda i: (i, 0)),
            pl.BlockSpec((tm, D_CP), lambda i: (i, 0)),
            pl.BlockSpec((None, D_MODEL, D_MODEL), lambda i: (layer, 0, 0),
                         pipeline_mode=pl.Buffered(1)),
        ],
        out_specs=pl.BlockSpec((tm, D_MODEL), lambda i: (i, 0)),
        out_shape=jax.ShapeDtypeStruct((t, D_MODEL), F32),
        compiler_params=pltpu.CompilerParams(
            dimension_semantics=("parallel",), vmem_limit_bytes=VMEM_LIMIT),
        name="outproj",
    )(x, y_attn, y_cp, w_out)


def kernel(x, ffn1_norm, ffn1_w_gate, ffn1_w_up, ffn1_w_down, mix_norm, w_in, conv_w, pool_w, pool_scale, w_out, ffn2_norm, ffn2_w_gate, ffn2_w_up, ffn2_w_down, final_norm):
    batch, seq, _ = x.shape
    depth = w_in.shape[0]
    assert seq % SUPER == 0 and seq % INPROJ_TM == 0 and (batch * seq) % FFN_TM == 0
    t = batch * seq
    x = x.reshape(t, D_MODEL)
    ffn1 = (ffn1_norm, ffn1_w_gate.astype(BF16), ffn1_w_up.astype(BF16), ffn1_w_down.astype(BF16))
    ffn2 = (ffn2_norm, ffn2_w_gate.astype(BF16), ffn2_w_up.astype(BF16), ffn2_w_down.astype(BF16))
    w_in, pool_w, w_out = w_in.astype(BF16), pool_w.astype(BF16), w_out.astype(BF16)
    for l in range(depth):
        x = _ffn(x, l, *ffn1)
        qkv, y_cp = _inproj(x, l, mix_norm, w_in, conv_w, pool_w, pool_scale, seq)
        y_attn = _attn(qkv.reshape(batch, seq, 3 * D_ATTN), batch, seq)
        x = _outproj(x, y_attn.reshape(t, D_ATTN), y_cp, l, w_out)
        x = _ffn(x, l, *ffn2, final_g=final_norm if l == depth - 1 else None)
    return x.reshape(batch, seq, D_MODEL)
```

```python
import functools

import jax
import jax.numpy as jnp
from jax import lax
from jax.experimental import pallas as pl
from jax.experimental.pallas import tpu as pltpu

D_MODEL = 2048
D_FF = 5632
D_ATTN = 1024
D_CONV = 512
D_POOL = 512
D_IN = 3 * D_ATTN + 3 * D_CONV + D_POOL
D_CP = D_CONV + D_POOL
HEAD_DIM = 64
CONV_WIDTH = 3
POOL_WINDOWS = (2, 4, 8, 16)
POOL_GROUP = D_POOL // len(POOL_WINDOWS)
DILATIONS = (1, 4, 16)
BLK = 128
RMS_EPS = 1e-6
NEG_INF = -1e30
FFN_RESIDUAL = 0.5

LANES = 128
HALO = 16
SUPER = BLK * max(DILATIONS)
HEADS_PER_STEP = LANES // HEAD_DIM

FFN_TM, FFN_TF = 1024, 512
FFN_CHUNK = 256
INPROJ_TM = 512
OUTPROJ_TM = 1024
VMEM_LIMIT = 60 * 1024 * 1024

F32 = jnp.float32
BF16 = jnp.bfloat16


def _rmsnorm(x, g):
    return x * lax.rsqrt(jnp.mean(x * x, axis=-1, keepdims=True) + RMS_EPS) * g


def _ffn_kernel(*refs, final):
    if final:
        x_ref, g_ref, wg_ref, wu_ref, wd_ref, fg_ref, o_ref, h_ref = refs
    else:
        x_ref, g_ref, wg_ref, wu_ref, wd_ref, o_ref, h_ref = refs
    j = pl.program_id(1)

    def accumulate(base_ref):
        h = h_ref[...]
        pieces = [slice(c, c + FFN_CHUNK) for c in range(0, FFN_TF, FFN_CHUNK)]
        gate_up = [(jnp.dot(h, wg_ref[:, cs].astype(BF16), preferred_element_type=F32),
                    jnp.dot(h, wu_ref[:, cs].astype(BF16), preferred_element_type=F32))
                   for cs in pieces]
        down = None
        for cs, (gate, up) in zip(pieces, gate_up):
            a = (gate * jax.nn.sigmoid(gate) * up * FFN_RESIDUAL).astype(BF16)
            d = jnp.dot(a, wd_ref[cs, :].astype(BF16), preferred_element_type=F32)
            down = d if down is None else down + d
        o_ref[...] = base_ref[...] + down

    @pl.when(j == 0)
    def _():
        h_ref[...] = _rmsnorm(x_ref[...], g_ref[...]).astype(BF16)
        accumulate(x_ref)

    @pl.when(j > 0)
    def _():
        accumulate(o_ref)

    if final:
        @pl.when(j == pl.num_programs(1) - 1)
        def _():
            o_ref[...] = _rmsnorm(o_ref[...], fg_ref[...])


def _ffn(x, layer, g, wg, wu, wd, final_g=None):
    t = x.shape[0]
    final = final_g is not None
    in_specs = [
        pl.BlockSpec((FFN_TM, D_MODEL), lambda i, j: (i, 0), pipeline_mode=pl.Buffered(1)),
        pl.BlockSpec((None, 1, D_MODEL), lambda i, j: (layer, 0, 0)),
        pl.BlockSpec((None, D_MODEL, FFN_TF), lambda i, j: (layer, 0, j)),
        pl.BlockSpec((None, D_MODEL, FFN_TF), lambda i, j: (layer, 0, j)),
        pl.BlockSpec((None, FFN_TF, D_MODEL), lambda i, j: (layer, j, 0)),
    ]
    args = [x, g[:, None, :], wg, wu, wd]
    if final:
        in_specs.append(pl.BlockSpec((1, D_MODEL), lambda i, j: (0, 0)))
        args.append(final_g.reshape(1, D_MODEL))
    return pl.pallas_call(
        functools.partial(_ffn_kernel, final=final),
        grid=(t // FFN_TM, D_FF // FFN_TF),
        in_specs=in_specs,
        out_specs=pl.BlockSpec((FFN_TM, D_MODEL), lambda i, j: (i, 0)),
        out_shape=jax.ShapeDtypeStruct((t, D_MODEL), F32),
        scratch_shapes=[pltpu.VMEM((FFN_TM, D_MODEL), BF16)],
        compiler_params=pltpu.CompilerParams(
            dimension_semantics=("parallel", "arbitrary"), vmem_limit_bytes=VMEM_LIMIT),
        name="ffn_final" if final else "ffn",
    )(*args)


def _inproj_kernel(x_ref, g_ref, w_ref, cw_ref, pw_ref, ps_ref, qkv_ref, ycp_ref,
                   ubuf, pbuf, *, tiles_per_seq):
    tm = x_ref.shape[0]
    i = pl.program_id(0)

    @pl.when(i % tiles_per_seq == 0)
    def _():
        ubuf[0:HALO, :] = jnp.zeros((HALO, D_CONV), F32)
        pbuf[0:HALO, :] = jnp.zeros((HALO, D_POOL), F32)

    h = _rmsnorm(x_ref[...], g_ref[...]).astype(BF16)

    c0 = 3 * D_ATTN
    gate_b = jnp.dot(h, w_ref[:, c0:c0 + D_CONV], preferred_element_type=F32)
    gate_c = jnp.dot(h, w_ref[:, c0 + D_CONV:c0 + 2 * D_CONV], preferred_element_type=F32)
    conv_in = jnp.dot(h, w_ref[:, c0 + 2 * D_CONV:c0 + 3 * D_CONV], preferred_element_type=F32)
    pool_in = jnp.dot(h, w_ref[:, c0 + 3 * D_CONV:], preferred_element_type=F32)
    u = gate_c * conv_in
    for c in range(3):
        cols = slice(c * D_ATTN, (c + 1) * D_ATTN)
        qkv_ref[:, cols] = jnp.dot(h, w_ref[:, cols], preferred_element_type=F32)

    ubuf[HALO:HALO + tm, :] = u
    pbuf[HALO:HALO + tm, :] = pool_in

    conv = cw_ref[CONV_WIDTH - 1:CONV_WIDTH, :] * u
    for k in range(1, CONV_WIDTH):
        conv += cw_ref[CONV_WIDTH - 1 - k:CONV_WIDTH - k, :] * ubuf[HALO - k:HALO - k + tm, :]
    ycp_ref[:, 0:D_CONV] = (gate_b * conv).astype(BF16)

    pos = (i % tiles_per_seq) * tm + lax.broadcasted_iota(jnp.int32, (tm, 1), 0)
    for gidx, w in enumerate(POOL_WINDOWS):
        cols = slice(gidx * POOL_GROUP, (gidx + 1) * POOL_GROUP)
        cur = pool_in[:, cols]
        win = cur
        for k in range(1, w):
            win += pbuf[HALO - k:HALO - k + tm, cols]
        count = jnp.minimum(pos + 1, w).astype(F32)
        pooled = (win / count - cur).astype(BF16)
        y = jnp.dot(pooled, pw_ref[gidx], preferred_element_type=F32) * ps_ref[:, cols]
        ycp_ref[:, D_CONV + gidx * POOL_GROUP:D_CONV + (gidx + 1) * POOL_GROUP] = y.astype(BF16)

    ubuf[0:HALO, :] = ubuf[tm:tm + HALO, :]
    pbuf[0:HALO, :] = pbuf[tm:tm + HALO, :]


def _inproj(x, layer, g, w_in, conv_w, pool_w, pool_scale, seq):
    t = x.shape[0]
    tm = INPROJ_TM
    const = lambda *shape: pl.BlockSpec((None,) + shape, lambda i: (layer,) + (0,) * len(shape),
                                        pipeline_mode=pl.Buffered(1))
    return pl.pallas_call(
        functools.partial(_inproj_kernel, tiles_per_seq=seq // tm),
        grid=(t // tm,),
        in_specs=[
            pl.BlockSpec((tm, D_MODEL), lambda i: (i, 0)),
            const(1, D_MODEL),
            const(D_MODEL, D_IN),
            const(CONV_WIDTH, D_CONV),
            const(len(POOL_WINDOWS), POOL_GROUP, POOL_GROUP),
            const(1, D_POOL),
        ],
        out_specs=[
            pl.BlockSpec((tm, 3 * D_ATTN), lambda i: (i, 0)),
            pl.BlockSpec((tm, D_CP), lambda i: (i, 0)),
        ],
        out_shape=[
            jax.ShapeDtypeStruct((t, 3 * D_ATTN), F32),
            jax.ShapeDtypeStruct((t, D_CP), BF16),
        ],
        scratch_shapes=[pltpu.VMEM((HALO + tm, D_CONV), F32),
                        pltpu.VMEM((HALO + tm, D_POOL), F32)],
        compiler_params=pltpu.CompilerParams(
            dimension_semantics=("arbitrary",), vmem_limit_bytes=VMEM_LIMIT),
        name="inproj",
    )(x, g[:, None, :], w_in, conv_w, pool_w, pool_scale[:, None, :])


def _attn_kernel(q_ref, kp_ref, kc_ref, vp_ref, vc_ref, o_ref, stat1_ref, stat4_ref,
                 q4_ref, k4_ref, v4_ref, s_ref, bias_ref, o4_ref):
    n = pl.program_id(2)
    quarter = SUPER // 4
    head_a = lax.broadcasted_iota(jnp.int32, (BLK, LANES), 1) < HEAD_DIM
    qi = lax.broadcasted_iota(jnp.int32, (HEADS_PER_STEP * BLK, 2 * BLK), 0) % BLK
    kj = lax.broadcasted_iota(jnp.int32, (HEADS_PER_STEP * BLK, 2 * BLK), 1)
    band = jnp.logical_and(kj >= qi, kj <= qi + BLK)
    band_prev = jnp.logical_and(band, jnp.logical_or(kj >= BLK, n > 0))
    bias_ref[0] = jnp.where(band, 0.0, NEG_INF)
    bias_ref[1] = jnp.where(band_prev, 0.0, NEG_INF)
    dims_nt = (((1,), (1,)), ((), ()))

    for r4 in range(4):
        by4 = pl.ds(r4, quarter, stride=4)
        q4_ref[r4] = q_ref[by4, :]
        k4_ref[r4, 0:quarter] = kp_ref[by4, :]
        k4_ref[r4, quarter:2 * quarter] = kc_ref[by4, :]
        v4_ref[r4, 0:quarter] = vp_ref[by4, :]
        v4_ref[r4, quarter:2 * quarter] = vc_ref[by4, :]

    def natural_kv(prev_ref, cur_ref, b):
        if b == 0:
            lo = prev_ref[SUPER - BLK:SUPER, :]
            return jnp.concatenate([lo, cur_ref[0:BLK, :]], axis=0)
        return cur_ref[BLK * (b - 1):BLK * (b + 1), :]

    blocks = [[], [], []]
    for b in range(SUPER // BLK):
        rows = slice(BLK * b, BLK * (b + 1))
        blocks[0].append((lambda rows=rows: q_ref[rows, :],
                          lambda b=b: natural_kv(kp_ref, kc_ref, b),
                          lambda b=b: natural_kv(vp_ref, vc_ref, b),
                          b == 0, (stat1_ref, (rows,))))
    for r4 in range(4):
        for b in range(quarter // BLK):
            rows = slice(BLK * b, BLK * (b + 1))
            krows = slice(quarter + BLK * (b - 1), quarter + BLK * (b + 1))
            blocks[1].append((lambda r4=r4, rows=rows: q4_ref[r4, rows, :],
                              lambda r4=r4, krows=krows: k4_ref[r4, krows, :],
                              lambda r4=r4, krows=krows: v4_ref[r4, krows, :],
                              b == 0, (stat4_ref, (0, r4, rows))))
    for r16 in range(16):
        r4, sub = r16 % 4, r16 // 4
        rows = pl.ds(sub, BLK, stride=4)
        krows = pl.ds(sub, 2 * BLK, stride=4)
        blocks[2].append((lambda r4=r4, rows=rows: q4_ref[r4, rows, :],
                          lambda r4=r4, krows=krows: k4_ref[r4, krows, :],
                          lambda r4=r4, krows=krows: v4_ref[r4, krows, :],
                          True, (stat4_ref, (1, r4, rows))))

    def split_heads(x):
        return jnp.where(head_a, x[:BLK], x[BLK:])

    for pat, pattern_blocks in enumerate(blocks):
        for c, (load_q, load_k, _, _, _) in enumerate(pattern_blocks):
            q = (load_q() * (HEAD_DIM ** -0.5)).astype(BF16)
            zero = jnp.zeros_like(q)
            q2 = jnp.concatenate([jnp.where(head_a, q, zero), jnp.where(head_a, zero, q)], axis=0)
            s_ref[pat, c] = lax.dot_general(q2, load_k().astype(BF16), dims_nt,
                                            preferred_element_type=F32)

        for c, (_, _, load_v, first, (stat_ref, where)) in enumerate(pattern_blocks):
            s = s_ref[pat, c] + bias_ref[1 if first else 0]
            m = jnp.max(jnp.maximum(s[:, :BLK], s[:, BLK:]), axis=1, keepdims=True)
            p = jnp.exp(s - m)
            l = jnp.sum(p[:, :BLK] + p[:, BLK:], axis=1, keepdims=True)
            pv2 = jnp.dot(p.astype(BF16), load_v().astype(BF16), preferred_element_type=F32)
            stat_ref[(0,) + where] = split_heads(pv2)
            stat_ref[(1,) + where] = split_heads(jnp.broadcast_to(m, (HEADS_PER_STEP * BLK, LANES)))
            stat_ref[(2,) + where] = split_heads(jnp.broadcast_to(l, (HEADS_PER_STEP * BLK, LANES)))

    for r4 in range(4):
        by4 = pl.ds(r4, quarter, stride=4)
        num, mx, den = zip((stat1_ref[0, by4, :], stat1_ref[1, by4, :], stat1_ref[2, by4, :]),
                           *[(stat4_ref[0, pat, r4], stat4_ref[1, pat, r4], stat4_ref[2, pat, r4])
                             for pat in range(2)])
        m_max = functools.reduce(jnp.maximum, mx)
        w = [jnp.exp(m - m_max) for m in mx]
        o4_ref[by4, :] = (sum(wi * ni for wi, ni in zip(w, num))
                          / sum(wi * di for wi, di in zip(w, den)))
    o_ref[...] = o4_ref[...].astype(o_ref.dtype)


def _attn(qkv, batch, seq):
    ns = seq // SUPER
    hg = D_ATTN // LANES
    blk = lambda imap: pl.BlockSpec((None, SUPER, LANES), imap)
    return pl.pallas_call(
        _attn_kernel,
        grid=(batch, hg, ns),
        in_specs=[
            blk(lambda b, g, n: (b, n, g)),
            blk(lambda b, g, n: (b, jnp.maximum(n - 1, 0), hg + g)),
            blk(lambda b, g, n: (b, n, hg + g)),
            blk(lambda b, g, n: (b, jnp.maximum(n - 1, 0), 2 * hg + g)),
            blk(lambda b, g, n: (b, n, 2 * hg + g)),
        ],
        out_specs=blk(lambda b, g, n: (b, n, g)),
        out_shape=jax.ShapeDtypeStruct((batch, seq, D_ATTN), BF16),
        scratch_shapes=[
            pltpu.VMEM((3, SUPER, LANES), F32),
            pltpu.VMEM((3, 2, 4, SUPER // 4, LANES), F32),
            pltpu.VMEM((4, SUPER // 4, LANES), F32),
            pltpu.VMEM((4, 2 * SUPER // 4, LANES), F32),
            pltpu.VMEM((4, 2 * SUPER // 4, LANES), F32),
            pltpu.VMEM((len(DILATIONS), SUPER // BLK, HEADS_PER_STEP * BLK, 2 * BLK), F32),
            pltpu.VMEM((2, HEADS_PER_STEP * BLK, 2 * BLK), F32),
            pltpu.VMEM((SUPER, LANES), F32)],
        compiler_params=pltpu.CompilerParams(
            dimension_semantics=("parallel", "parallel", "arbitrary"),
            vmem_limit_bytes=VMEM_LIMIT),
        name="attn",
    )(qkv, qkv, qkv, qkv, qkv)


def _outproj_kernel(x_ref, a_ref, cp_ref, w_ref, o_ref):
    o_ref[...] = (x_ref[...]
                  + jnp.dot(a_ref[...], w_ref[0:D_ATTN, :], preferred_element_type=F32)
                  + jnp.dot(cp_ref[...], w_ref[D_ATTN:, :], preferred_element_type=F32))


def _outproj(x, y_attn, y_cp, layer, w_out):
    t = x.shape[0]
    tm = OUTPROJ_TM
    return pl.pallas_call(
        _outproj_kernel,
        grid=(t // tm,),
        in_specs=[
            pl.BlockSpec((tm, D_MODEL), lambda i: (i, 0)),
            pl.BlockSpec((tm, D_ATTN), lambda i: (i, 0)),
            pl.BlockSpec((tm, D_CP), lambda i: (i, 0)),
            pl.BlockSpec((None, D_MODEL, D_MODEL), lambda i: (layer, 0, 0),
                         pipeline_mode=pl.Buffered(1)),
        ],
        out_specs=pl.BlockSpec((tm, D_MODEL), lambda i: (i, 0)),
        out_shape=jax.ShapeDtypeStruct((t, D_MODEL), F32),
        compiler_params=pltpu.CompilerParams(
            dimension_semantics=("parallel",), vmem_limit_bytes=VMEM_LIMIT),
        name="outproj",
    )(x, y_attn, y_cp, w_out)


def kernel(x, ffn1_norm, ffn1_w_gate, ffn1_w_up, ffn1_w_down, mix_norm, w_in, conv_w, pool_w, pool_scale, w_out, ffn2_norm, ffn2_w_gate, ffn2_w_up, ffn2_w_down, final_norm):
    batch, seq, _ = x.shape
    depth = w_in.shape[0]
    assert seq % SUPER == 0 and seq % INPROJ_TM == 0 and (batch * seq) % FFN_TM == 0
    t = batch * seq
    x = x.reshape(t, D_MODEL)
    ffn1 = (ffn1_norm, ffn1_w_gate, ffn1_w_up, ffn1_w_down)
    ffn2 = (ffn2_norm, ffn2_w_gate, ffn2_w_up, ffn2_w_down)
    w_in, pool_w, w_out = w_in.astype(BF16), pool_w.astype(BF16), w_out.astype(BF16)
    for l in range(depth):
        x = _ffn(x, l, *ffn1)
        qkv, y_cp = _inproj(x, l, mix_norm, w_in, conv_w, pool_w, pool_scale, seq)
        y_attn = _attn(qkv.reshape(batch, seq, 3 * D_ATTN), batch, seq)
        x = _outproj(x, y_attn.reshape(t, D_ATTN), y_cp, l, w_out)
        x = _ffn(x, l, *ffn2, final_g=final_norm if l == depth - 1 else None)
    return x.reshape(batch, seq, D_MODEL)
```

```python
import functools

import jax
import jax.numpy as jnp
from jax import lax
from jax.experimental import pallas as pl
from jax.experimental.pallas import tpu as pltpu

D_MODEL = 2048
D_FF = 5632
D_ATTN = 1024
D_CONV = 512
D_POOL = 512
D_IN = 3 * D_ATTN + 3 * D_CONV + D_POOL
D_CP = D_CONV + D_POOL
HEAD_DIM = 64
CONV_WIDTH = 3
POOL_WINDOWS = (2, 4, 8, 16)
POOL_GROUP = D_POOL // len(POOL_WINDOWS)
DILATIONS = (1, 4, 16)
BLK = 128
RMS_EPS = 1e-6
NEG_INF = -1e30
FFN_RESIDUAL = 0.5

LANES = 128
HALO = 16
SUPER = BLK * max(DILATIONS)
HEADS_PER_STEP = LANES // HEAD_DIM
ATTN_AHEAD = 4

FFN_TM, FFN_TF = 1024, 512
FFN_CHUNK = 256
INPROJ_TM = 512
OUTPROJ_TM = 1024
VMEM_LIMIT = 60 * 1024 * 1024

F32 = jnp.float32
BF16 = jnp.bfloat16


def _rmsnorm(x, g):
    return x * lax.rsqrt(jnp.mean(x * x, axis=-1, keepdims=True) + RMS_EPS) * g


def _ffn_kernel(*refs, final):
    if final:
        x_ref, g_ref, wg_ref, wu_ref, wd_ref, fg_ref, o_ref, h_ref = refs
    else:
        x_ref, g_ref, wg_ref, wu_ref, wd_ref, o_ref, h_ref = refs
    j = pl.program_id(1)

    def accumulate(base_ref):
        h = h_ref[...]
        pieces = [slice(c, c + FFN_CHUNK) for c in range(0, FFN_TF, FFN_CHUNK)]
        gate_up = [(jnp.dot(h, wg_ref[:, cs].astype(BF16), preferred_element_type=F32),
                    jnp.dot(h, wu_ref[:, cs].astype(BF16), preferred_element_type=F32))
                   for cs in pieces]
        down = None
        for cs, (gate, up) in zip(pieces, gate_up):
            a = (gate * jax.nn.sigmoid(gate) * up * FFN_RESIDUAL).astype(BF16)
            d = jnp.dot(a, wd_ref[cs, :].astype(BF16), preferred_element_type=F32)
            down = d if down is None else down + d
        o_ref[...] = base_ref[...] + down

    @pl.when(j == 0)
    def _():
        h_ref[...] = _rmsnorm(x_ref[...], g_ref[...]).astype(BF16)
        accumulate(x_ref)

    @pl.when(j > 0)
    def _():
        accumulate(o_ref)

    if final:
        @pl.when(j == pl.num_programs(1) - 1)
        def _():
            o_ref[...] = _rmsnorm(o_ref[...], fg_ref[...])


def _ffn(x, layer, g, wg, wu, wd, final_g=None):
    t = x.shape[0]
    final = final_g is not None
    in_specs = [
        pl.BlockSpec((FFN_TM, D_MODEL), lambda i, j: (i, 0), pipeline_mode=pl.Buffered(1)),
        pl.BlockSpec((None, 1, D_MODEL), lambda i, j: (layer, 0, 0)),
        pl.BlockSpec((None, D_MODEL, FFN_TF), lambda i, j: (layer, 0, j)),
        pl.BlockSpec((None, D_MODEL, FFN_TF), lambda i, j: (layer, 0, j)),
        pl.BlockSpec((None, FFN_TF, D_MODEL), lambda i, j: (layer, j, 0)),
    ]
    args = [x, g[:, None, :], wg, wu, wd]
    if final:
        in_specs.append(pl.BlockSpec((1, D_MODEL), lambda i, j: (0, 0)))
        args.append(final_g.reshape(1, D_MODEL))
    return pl.pallas_call(
        functools.partial(_ffn_kernel, final=final),
        grid=(t // FFN_TM, D_FF // FFN_TF),
        in_specs=in_specs,
        out_specs=pl.BlockSpec((FFN_TM, D_MODEL), lambda i, j: (i, 0)),
        out_shape=jax.ShapeDtypeStruct((t, D_MODEL), F32),
        scratch_shapes=[pltpu.VMEM((FFN_TM, D_MODEL), BF16)],
        compiler_params=pltpu.CompilerParams(
            dimension_semantics=("parallel", "arbitrary"), vmem_limit_bytes=VMEM_LIMIT),
        name="ffn_final" if final else "ffn",
    )(*args)


def _inproj_kernel(x_ref, g_ref, w_ref, cw_ref, pw_ref, ps_ref, qkv_ref, ycp_ref,
                   ubuf, pbuf, *, tiles_per_seq):
    tm = x_ref.shape[0]
    i = pl.program_id(0)

    @pl.when(i % tiles_per_seq == 0)
    def _():
        ubuf[0:HALO, :] = jnp.zeros((HALO, D_CONV), F32)
        pbuf[0:HALO, :] = jnp.zeros((HALO, D_POOL), F32)

    h = _rmsnorm(x_ref[...], g_ref[...]).astype(BF16)

    c0 = 3 * D_ATTN
    gate_b = jnp.dot(h, w_ref[:, c0:c0 + D_CONV], preferred_element_type=F32)
    gate_c = jnp.dot(h, w_ref[:, c0 + D_CONV:c0 + 2 * D_CONV], preferred_element_type=F32)
    conv_in = jnp.dot(h, w_ref[:, c0 + 2 * D_CONV:c0 + 3 * D_CONV], preferred_element_type=F32)
    pool_in = jnp.dot(h, w_ref[:, c0 + 3 * D_CONV:], preferred_element_type=F32)
    u = gate_c * conv_in
    for c in range(3):
        cols = slice(c * D_ATTN, (c + 1) * D_ATTN)
        qkv_ref[:, cols] = jnp.dot(h, w_ref[:, cols], preferred_element_type=F32)

    ubuf[HALO:HALO + tm, :] = u
    pbuf[HALO:HALO + tm, :] = pool_in

    conv = cw_ref[CONV_WIDTH - 1:CONV_WIDTH, :] * u
    for k in range(1, CONV_WIDTH):
        conv += cw_ref[CONV_WIDTH - 1 - k:CONV_WIDTH - k, :] * ubuf[HALO - k:HALO - k + tm, :]
    ycp_ref[:, 0:D_CONV] = (gate_b * conv).astype(BF16)

    pos = (i % tiles_per_seq) * tm + lax.broadcasted_iota(jnp.int32, (tm, 1), 0)
    for gidx, w in enumerate(POOL_WINDOWS):
        cols = slice(gidx * POOL_GROUP, (gidx + 1) * POOL_GROUP)
        cur = pool_in[:, cols]
        win = cur
        for k in range(1, w):
            win += pbuf[HALO - k:HALO - k + tm, cols]
        count = jnp.minimum(pos + 1, w).astype(F32)
        pooled = (win / count - cur).astype(BF16)
        y = jnp.dot(pooled, pw_ref[gidx], preferred_element_type=F32) * ps_ref[:, cols]
        ycp_ref[:, D_CONV + gidx * POOL_GROUP:D_CONV + (gidx + 1) * POOL_GROUP] = y.astype(BF16)

    ubuf[0:HALO, :] = ubuf[tm:tm + HALO, :]
    pbuf[0:HALO, :] = pbuf[tm:tm + HALO, :]


def _inproj(x, layer, g, w_in, conv_w, pool_w, pool_scale, seq):
    t = x.shape[0]
    tm = INPROJ_TM
    const = lambda *shape: pl.BlockSpec((None,) + shape, lambda i: (layer,) + (0,) * len(shape),
                                        pipeline_mode=pl.Buffered(1))
    return pl.pallas_call(
        functools.partial(_inproj_kernel, tiles_per_seq=seq // tm),
        grid=(t // tm,),
        in_specs=[
            pl.BlockSpec((tm, D_MODEL), lambda i: (i, 0)),
            const(1, D_MODEL),
            const(D_MODEL, D_IN),
            const(CONV_WIDTH, D_CONV),
            const(len(POOL_WINDOWS), POOL_GROUP, POOL_GROUP),
            const(1, D_POOL),
        ],
        out_specs=[
            pl.BlockSpec((tm, 3 * D_ATTN), lambda i: (i, 0)),
            pl.BlockSpec((tm, D_CP), lambda i: (i, 0)),
        ],
        out_shape=[
            jax.ShapeDtypeStruct((t, 3 * D_ATTN), F32),
            jax.ShapeDtypeStruct((t, D_CP), BF16),
        ],
        scratch_shapes=[pltpu.VMEM((HALO + tm, D_CONV), F32),
                        pltpu.VMEM((HALO + tm, D_POOL), F32)],
        compiler_params=pltpu.CompilerParams(
            dimension_semantics=("arbitrary",), vmem_limit_bytes=VMEM_LIMIT),
        name="inproj",
    )(x, g[:, None, :], w_in, conv_w, pool_w, pool_scale[:, None, :])


def _attn_kernel(q_ref, kp_ref, kc_ref, vp_ref, vc_ref, o_ref, stat1_ref, stat4_ref,
                 q4_ref, k4_ref, v4_ref, s_ref, bias_ref, o4_ref):
    n = pl.program_id(2)
    quarter = SUPER // 4
    head_a = lax.broadcasted_iota(jnp.int32, (BLK, LANES), 1) < HEAD_DIM
    qi = lax.broadcasted_iota(jnp.int32, (HEADS_PER_STEP * BLK, 2 * BLK), 0) % BLK
    kj = lax.broadcasted_iota(jnp.int32, (HEADS_PER_STEP * BLK, 2 * BLK), 1)
    band = jnp.logical_and(kj >= qi, kj <= qi + BLK)
    band_prev = jnp.logical_and(band, jnp.logical_or(kj >= BLK, n > 0))
    bias_ref[0] = jnp.where(band, 0.0, NEG_INF)
    bias_ref[1] = jnp.where(band_prev, 0.0, NEG_INF)
    dims_nt = (((1,), (1,)), ((), ()))

    for r4 in range(4):
        by4 = pl.ds(r4, quarter, stride=4)
        q4_ref[r4] = q_ref[by4, :]
        k4_ref[r4, 0:quarter] = kp_ref[by4, :]
        k4_ref[r4, quarter:2 * quarter] = kc_ref[by4, :]
        v4_ref[r4, 0:quarter] = vp_ref[by4, :]
        v4_ref[r4, quarter:2 * quarter] = vc_ref[by4, :]

    def natural_kv(prev_ref, cur_ref, b):
        if b == 0:
            lo = prev_ref[SUPER - BLK:SUPER, :]
            return jnp.concatenate([lo, cur_ref[0:BLK, :]], axis=0)
        return cur_ref[BLK * (b - 1):BLK * (b + 1), :]

    blocks = [[], [], []]
    for b in range(SUPER // BLK):
        rows = slice(BLK * b, BLK * (b + 1))
        blocks[0].append((lambda rows=rows: q_ref[rows, :],
                          lambda b=b: natural_kv(kp_ref, kc_ref, b),
                          lambda b=b: natural_kv(vp_ref, vc_ref, b),
                          b == 0, (stat1_ref, (rows,))))
    for r4 in range(4):
        for b in range(quarter // BLK):
            rows = slice(BLK * b, BLK * (b + 1))
            krows = slice(quarter + BLK * (b - 1), quarter + BLK * (b + 1))
            blocks[1].append((lambda r4=r4, rows=rows: q4_ref[r4, rows, :],
                              lambda r4=r4, krows=krows: k4_ref[r4, krows, :],
                              lambda r4=r4, krows=krows: v4_ref[r4, krows, :],
                              b == 0, (stat4_ref, (0, r4, rows))))
    for r16 in range(16):
        r4, sub = r16 % 4, r16 // 4
        rows = pl.ds(sub, BLK, stride=4)
        krows = pl.ds(sub, 2 * BLK, stride=4)
        blocks[2].append((lambda r4=r4, rows=rows: q4_ref[r4, rows, :],
                          lambda r4=r4, krows=krows: k4_ref[r4, krows, :],
                          lambda r4=r4, krows=krows: v4_ref[r4, krows, :],
                          True, (stat4_ref, (1, r4, rows))))

    def split_heads(x):
        return jnp.where(head_a, x[:BLK], x[BLK:])

    def scores(pat, c):
        load_q, load_k = blocks[pat][c][:2]
        q = (load_q() * (HEAD_DIM ** -0.5)).astype(BF16)
        zero = jnp.zeros_like(q)
        q2 = jnp.concatenate([jnp.where(head_a, q, zero), jnp.where(head_a, zero, q)], axis=0)
        s_ref[pat, c] = lax.dot_general(q2, load_k().astype(BF16), dims_nt,
                                        preferred_element_type=F32)

    def softmax_pv(pat, c):
        _, _, load_v, first, (stat_ref, where) = blocks[pat][c]
        s = s_ref[pat, c] + bias_ref[1 if first else 0]
        m = jnp.max(jnp.maximum(s[:, :BLK], s[:, BLK:]), axis=1, keepdims=True)
        p = jnp.exp(s - m)
        l = jnp.sum(p[:, :BLK] + p[:, BLK:], axis=1, keepdims=True)
        pv2 = jnp.dot(p.astype(BF16), load_v().astype(BF16), preferred_element_type=F32)
        stat_ref[(0,) + where] = split_heads(pv2)
        stat_ref[(1,) + where] = split_heads(jnp.broadcast_to(m, (HEADS_PER_STEP * BLK, LANES)))
        stat_ref[(2,) + where] = split_heads(jnp.broadcast_to(l, (HEADS_PER_STEP * BLK, LANES)))

    order = [(pat, c) for pat in range(len(blocks)) for c in range(len(blocks[pat]))]
    for pos in range(len(order) + ATTN_AHEAD):
        if pos < len(order):
            scores(*order[pos])
        if pos >= ATTN_AHEAD:
            softmax_pv(*order[pos - ATTN_AHEAD])

    for r4 in range(4):
        by4 = pl.ds(r4, quarter, stride=4)
        num, mx, den = zip((stat1_ref[0, by4, :], stat1_ref[1, by4, :], stat1_ref[2, by4, :]),
                           *[(stat4_ref[0, pat, r4], stat4_ref[1, pat, r4], stat4_ref[2, pat, r4])
                             for pat in range(2)])
        m_max = functools.reduce(jnp.maximum, mx)
        w = [jnp.exp(m - m_max) for m in mx]
        o4_ref[by4, :] = (sum(wi * ni for wi, ni in zip(w, num))
                          / sum(wi * di for wi, di in zip(w, den)))
    o_ref[...] = o4_ref[...].astype(o_ref.dtype)


def _attn(qkv, batch, seq):
    ns = seq // SUPER
    hg = D_ATTN // LANES
    blk = lambda imap: pl.BlockSpec((None, SUPER, LANES), imap)
    return pl.pallas_call(
        _attn_kernel,
        grid=(batch, hg, ns),
        in_specs=[
            blk(lambda b, g, n: (b, n, g)),
            blk(lambda b, g, n: (b, jnp.maximum(n - 1, 0), hg + g)),
            blk(lambda b, g, n: (b, n, hg + g)),
            blk(lambda b, g, n: (b, jnp.maximum(n - 1, 0), 2 * hg + g)),
            blk(lambda b, g, n: (b, n, 2 * hg + g)),
        ],
        out_specs=blk(lambda b, g, n: (b, n, g)),
        out_shape=jax.ShapeDtypeStruct((batch, seq, D_ATTN), BF16),
        scratch_shapes=[
            pltpu.VMEM((3, SUPER, LANES), F32),
            pltpu.VMEM((3, 2, 4, SUPER // 4, LANES), F32),
            pltpu.VMEM((4, SUPER // 4, LANES), F32),
            pltpu.VMEM((4, 2 * SUPER // 4, LANES), F32),
            pltpu.VMEM((4, 2 * SUPER // 4, LANES), F32),
            pltpu.VMEM((len(DILATIONS), SUPER // BLK, HEADS_PER_STEP * BLK, 2 * BLK), F32),
            pltpu.VMEM((2, HEADS_PER_STEP * BLK, 2 * BLK), F32),
            pltpu.VMEM((SUPER, LANES), F32)],
        compiler_params=pltpu.CompilerParams(
            dimension_semantics=("parallel", "parallel", "arbitrary"),
            vmem_limit_bytes=VMEM_LIMIT),
        name="attn",
    )(qkv, qkv, qkv, qkv, qkv)


def _outproj_kernel(x_ref, a_ref, cp_ref, w_ref, o_ref):
    o_ref[...] = (x_ref[...]
                  + jnp.dot(a_ref[...], w_ref[0:D_ATTN, :], preferred_element_type=F32)
                  + jnp.dot(cp_ref[...], w_ref[D_ATTN:, :], preferred_element_type=F32))


def _outproj(x, y_attn, y_cp, layer, w_out):
    t = x.shape[0]
    tm = OUTPROJ_TM
    return pl.pallas_call(
        _outproj_kernel,
        grid=(t // tm,),
        in_specs=[
            pl.BlockSpec((tm, D_MODEL), lambda i: (i, 0)),
            pl.BlockSpec((tm, D_ATTN), lambda i: (i, 0)),
            pl.BlockSpec((tm, D_CP), lambda i: (i, 0)),
            pl.BlockSpec((None, D_MODEL, D_MODEL), lambda i: (layer, 0, 0),
                         pipeline_mode=pl.Buffered(1)),
        ],
        out_specs=pl.BlockSpec((tm, D_MODEL), lambda i: (i, 0)),
        out_shape=jax.ShapeDtypeStruct((t, D_MODEL), F32),
        compiler_params=pltpu.CompilerParams(
            dimension_semantics=("parallel",), vmem_limit_bytes=VMEM_LIMIT),
        name="outproj",
    )(x, y_attn, y_cp, w_out)


def kernel(x, ffn1_norm, ffn1_w_gate, ffn1_w_up, ffn1_w_down, mix_norm, w_in, conv_w, pool_w, pool_scale, w_out, ffn2_norm, ffn2_w_gate, ffn2_w_up, ffn2_w_down, final_norm):
    batch, seq, _ = x.shape
    depth = w_in.shape[0]
    assert seq % SUPER == 0 and seq % INPROJ_TM == 0 and (batch * seq) % FFN_TM == 0
    t = batch * seq
    x = x.reshape(t, D_MODEL)
    ffn1 = (ffn1_norm, ffn1_w_gate, ffn1_w_up, ffn1_w_down)
    ffn2 = (ffn2_norm, ffn2_w_gate, ffn2_w_up, ffn2_w_down)
    w_in, pool_w, w_out = w_in.astype(BF16), pool_w.astype(BF16), w_out.astype(BF16)
    for l in range(depth):
        x = _ffn(x, l, *ffn1)
        qkv, y_cp = _inproj(x, l, mix_norm, w_in, conv_w, pool_w, pool_scale, seq)
        y_attn = _attn(qkv.reshape(batch, seq, 3 * D_ATTN), batch, seq)
        x = _outproj(x, y_attn.reshape(t, D_ATTN), y_cp, l, w_out)
        x = _ffn(x, l, *ffn2, final_g=final_norm if l == depth - 1 else None)
    return x.reshape(batch, seq, D_MODEL)
```

```python
import functools

import jax
import jax.numpy as jnp
from jax import lax
from jax.experimental import pallas as pl
from jax.experimental.pallas import tpu as pltpu

D_MODEL = 2048
D_FF = 5632
D_ATTN = 1024
D_CONV = 512
D_POOL = 512
D_IN = 3 * D_ATTN + 3 * D_CONV + D_POOL
D_CP = D_CONV + D_POOL
HEAD_DIM = 64
CONV_WIDTH = 3
POOL_WINDOWS = (2, 4, 8, 16)
POOL_GROUP = D_POOL // len(POOL_WINDOWS)
DILATIONS = (1, 4, 16)
BLK = 128
RMS_EPS = 1e-6
NEG_INF = -1e30
FFN_RESIDUAL = 0.5

LANES = 128
HALO = 16
SUPER = BLK * max(DILATIONS)
HEADS_PER_STEP = LANES // HEAD_DIM
ATTN_AHEAD = 4

FFN_TM, FFN_TF = 1024, 512
FFN_CHUNK = 256
FFN_PREFETCH_STEP = 2
INPROJ_TM = 512
OUTPROJ_TM = 1024
VMEM_LIMIT = 60 * 1024 * 1024

F32 = jnp.float32
BF16 = jnp.bfloat16


def _rmsnorm(x, g):
    return x * lax.rsqrt(jnp.mean(x * x, axis=-1, keepdims=True) + RMS_EPS) * g


def _ffn_kernel(*refs, final, n_tiles):
    if final:
        x_hbm, g_ref, wg_ref, wu_ref, wd_ref, fg_ref, o_hbm, acc_ref, h_ref, x_sem, o_sem = refs
    else:
        x_hbm, g_ref, wg_ref, wu_ref, wd_ref, o_hbm, acc_ref, h_ref, x_sem, o_sem = refs
    i = pl.program_id(0)
    j = pl.program_id(1)
    tm = acc_ref.shape[1]

    def tile_rows(tile):
        return pl.ds(pl.multiple_of(tile * tm, tm), tm)

    def x_copy(tile, slot):
        return pltpu.make_async_copy(x_hbm.at[tile_rows(tile), :], acc_ref.at[slot], x_sem.at[slot])

    def o_copy(tile, slot):
        return pltpu.make_async_copy(acc_ref.at[slot], o_hbm.at[tile_rows(tile), :], o_sem.at[slot])

    def tile_step(slot):
        acc = acc_ref.at[slot]

        def accumulate():
            h = h_ref[...]
            pieces = [slice(c, c + FFN_CHUNK) for c in range(0, FFN_TF, FFN_CHUNK)]
            gate_up = [(jnp.dot(h, wg_ref[:, cs].astype(BF16), preferred_element_type=F32),
                        jnp.dot(h, wu_ref[:, cs].astype(BF16), preferred_element_type=F32))
                       for cs in pieces]
            down = None
            for cs, (gate, up) in zip(pieces, gate_up):
                a = (gate * jax.nn.sigmoid(gate) * up * FFN_RESIDUAL).astype(BF16)
                d = jnp.dot(a, wd_ref[cs, :].astype(BF16), preferred_element_type=F32)
                down = d if down is None else down + d
            acc[...] += down

        @pl.when(j == 0)
        def _():
            @pl.when(i == 0)
            def _():
                x_copy(0, slot).start()

            x_copy(i, slot).wait()
            h_ref[...] = _rmsnorm(acc[...], g_ref[...]).astype(BF16)
            accumulate()

        @pl.when(j > 0)
        def _():
            accumulate()

        @pl.when(j == FFN_PREFETCH_STEP)
        def _():
            @pl.when(i >= 1)
            def _():
                o_copy(i - 1, 1 - slot).wait()

            @pl.when(i + 1 < n_tiles)
            def _():
                x_copy(i + 1, 1 - slot).start()

        @pl.when(j == pl.num_programs(1) - 1)
        def _():
            if final:
                acc[...] = _rmsnorm(acc[...], fg_ref[...])
            o_copy(i, slot).start()

            @pl.when(i == n_tiles - 1)
            def _():
                o_copy(i, slot).wait()

    for slot in range(2):
        pl.when(i % 2 == slot)(functools.partial(tile_step, slot))


def _ffn(x, layer, g, wg, wu, wd, final_g=None):
    t = x.shape[0]
    final = final_g is not None
    n_tiles, steps = t // FFN_TM, D_FF // FFN_TF
    assert steps > FFN_PREFETCH_STEP >= 1
    in_specs = [
        pl.BlockSpec(memory_space=pl.ANY),
        pl.BlockSpec((None, 1, D_MODEL), lambda i, j: (layer, 0, 0)),
        pl.BlockSpec((None, D_MODEL, FFN_TF), lambda i, j: (layer, 0, j)),
        pl.BlockSpec((None, D_MODEL, FFN_TF), lambda i, j: (layer, 0, j)),
        pl.BlockSpec((None, FFN_TF, D_MODEL), lambda i, j: (layer, j, 0)),
    ]
    args = [x, g[:, None, :], wg, wu, wd]
    if final:
        in_specs.append(pl.BlockSpec((1, D_MODEL), lambda i, j: (0, 0)))
        args.append(final_g.reshape(1, D_MODEL))
    return pl.pallas_call(
        functools.partial(_ffn_kernel, final=final, n_tiles=n_tiles),
        grid=(n_tiles, steps),
        in_specs=in_specs,
        out_specs=pl.BlockSpec(memory_space=pl.ANY),
        out_shape=jax.ShapeDtypeStruct((t, D_MODEL), F32),
        scratch_shapes=[pltpu.VMEM((2, FFN_TM, D_MODEL), F32),
                        pltpu.VMEM((FFN_TM, D_MODEL), BF16),
                        pltpu.SemaphoreType.DMA((2,)),
                        pltpu.SemaphoreType.DMA((2,))],
        compiler_params=pltpu.CompilerParams(
            dimension_semantics=("arbitrary", "arbitrary"), vmem_limit_bytes=VMEM_LIMIT),
        name="ffn_final" if final else "ffn",
    )(*args)


def _inproj_kernel(x_ref, g_ref, w_ref, cw_ref, pw_ref, ps_ref, qkv_ref, ycp_ref,
                   ubuf, pbuf, *, tiles_per_seq):
    tm = x_ref.shape[0]
    i = pl.program_id(0)

    @pl.when(i % tiles_per_seq == 0)
    def _():
        ubuf[0:HALO, :] = jnp.zeros((HALO, D_CONV), F32)
        pbuf[0:HALO, :] = jnp.zeros((HALO, D_POOL), F32)

    h = _rmsnorm(x_ref[...], g_ref[...]).astype(BF16)

    c0 = 3 * D_ATTN
    gate_b = jnp.dot(h, w_ref[:, c0:c0 + D_CONV], preferred_element_type=F32)
    gate_c = jnp.dot(h, w_ref[:, c0 + D_CONV:c0 + 2 * D_CONV], preferred_element_type=F32)
    conv_in = jnp.dot(h, w_ref[:, c0 + 2 * D_CONV:c0 + 3 * D_CONV], preferred_element_type=F32)
    pool_in = jnp.dot(h, w_ref[:, c0 + 3 * D_CONV:], preferred_element_type=F32)
    u = gate_c * conv_in
    for c in range(3):
        cols = slice(c * D_ATTN, (c + 1) * D_ATTN)
        qkv_ref[:, cols] = jnp.dot(h, w_ref[:, cols], preferred_element_type=F32)

    ubuf[HALO:HALO + tm, :] = u
    pbuf[HALO:HALO + tm, :] = pool_in

    conv = cw_ref[CONV_WIDTH - 1:CONV_WIDTH, :] * u
    for k in range(1, CONV_WIDTH):
        conv += cw_ref[CONV_WIDTH - 1 - k:CONV_WIDTH - k, :] * ubuf[HALO - k:HALO - k + tm, :]
    ycp_ref[:, 0:D_CONV] = (gate_b * conv).astype(BF16)

    pos = (i % tiles_per_seq) * tm + lax.broadcasted_iota(jnp.int32, (tm, 1), 0)
    for gidx, w in enumerate(POOL_WINDOWS):
        cols = slice(gidx * POOL_GROUP, (gidx + 1) * POOL_GROUP)
        cur = pool_in[:, cols]
        win = cur
        for k in range(1, w):
            win += pbuf[HALO - k:HALO - k + tm, cols]
        count = jnp.minimum(pos + 1, w).astype(F32)
        pooled = (win / count - cur).astype(BF16)
        y = jnp.dot(pooled, pw_ref[gidx], preferred_element_type=F32) * ps_ref[:, cols]
        ycp_ref[:, D_CONV + gidx * POOL_GROUP:D_CONV + (gidx + 1) * POOL_GROUP] = y.astype(BF16)

    ubuf[0:HALO, :] = ubuf[tm:tm + HALO, :]
    pbuf[0:HALO, :] = pbuf[tm:tm + HALO, :]


def _inproj(x, layer, g, w_in, conv_w, pool_w, pool_scale, seq):
    t = x.shape[0]
    tm = INPROJ_TM
    const = lambda *shape: pl.BlockSpec((None,) + shape, lambda i: (layer,) + (0,) * len(shape),
                                        pipeline_mode=pl.Buffered(1))
    return pl.pallas_call(
        functools.partial(_inproj_kernel, tiles_per_seq=seq // tm),
        grid=(t // tm,),
        in_specs=[
            pl.BlockSpec((tm, D_MODEL), lambda i: (i, 0)),
            const(1, D_MODEL),
            const(D_MODEL, D_IN),
            const(CONV_WIDTH, D_CONV),
            const(len(POOL_WINDOWS), POOL_GROUP, POOL_GROUP),
            const(1, D_POOL),
        ],
        out_specs=[
            pl.BlockSpec((tm, 3 * D_ATTN), lambda i: (i, 0)),
            pl.BlockSpec((tm, D_CP), lambda i: (i, 0)),
        ],
        out_shape=[
            jax.ShapeDtypeStruct((t, 3 * D_ATTN), F32),
            jax.ShapeDtypeStruct((t, D_CP), BF16),
        ],
        scratch_shapes=[pltpu.VMEM((HALO + tm, D_CONV), F32),
                        pltpu.VMEM((HALO + tm, D_POOL), F32)],
        compiler_params=pltpu.CompilerParams(
            dimension_semantics=("arbitrary",), vmem_limit_bytes=VMEM_LIMIT),
        name="inproj",
    )(x, g[:, None, :], w_in, conv_w, pool_w, pool_scale[:, None, :])


def _attn_kernel(q_ref, kp_ref, kc_ref, vp_ref, vc_ref, o_ref, stat1_ref, stat4_ref,
                 q4_ref, k4_ref, v4_ref, s_ref, bias_ref, o4_ref):
    n = pl.program_id(2)
    quarter = SUPER // 4
    head_a = lax.broadcasted_iota(jnp.int32, (BLK, LANES), 1) < HEAD_DIM
    qi = lax.broadcasted_iota(jnp.int32, (HEADS_PER_STEP * BLK, 2 * BLK), 0) % BLK
    kj = lax.broadcasted_iota(jnp.int32, (HEADS_PER_STEP * BLK, 2 * BLK), 1)
    band = jnp.logical_and(kj >= qi, kj <= qi + BLK)
    band_prev = jnp.logical_and(band, jnp.logical_or(kj >= BLK, n > 0))
    bias_ref[0] = jnp.where(band, 0.0, NEG_INF)
    bias_ref[1] = jnp.where(band_prev, 0.0, NEG_INF)
    dims_nt = (((1,), (1,)), ((), ()))

    for r4 in range(4):
        by4 = pl.ds(r4, quarter, stride=4)
        q4_ref[r4] = q_ref[by4, :]
        k4_ref[r4, 0:quarter] = kp_ref[by4, :]
        k4_ref[r4, quarter:2 * quarter] = kc_ref[by4, :]
        v4_ref[r4, 0:quarter] = vp_ref[by4, :]
        v4_ref[r4, quarter:2 * quarter] = vc_ref[by4, :]

    def natural_kv(prev_ref, cur_ref, b):
        if b == 0:
            lo = prev_ref[SUPER - BLK:SUPER, :]
            return jnp.concatenate([lo, cur_ref[0:BLK, :]], axis=0)
        return cur_ref[BLK * (b - 1):BLK * (b + 1), :]

    blocks = [[], [], []]
    for b in range(SUPER // BLK):
        rows = slice(BLK * b, BLK * (b + 1))
        blocks[0].append((lambda rows=rows: q_ref[rows, :],
                          lambda b=b: natural_kv(kp_ref, kc_ref, b),
                          lambda b=b: natural_kv(vp_ref, vc_ref, b),
                          b == 0, (stat1_ref, (rows,))))
    for r4 in range(4):
        for b in range(quarter // BLK):
            rows = slice(BLK * b, BLK * (b + 1))
            krows = slice(quarter + BLK * (b - 1), quarter + BLK * (b + 1))
            blocks[1].append((lambda r4=r4, rows=rows: q4_ref[r4, rows, :],
                              lambda r4=r4, krows=krows: k4_ref[r4, krows, :],
                              lambda r4=r4, krows=krows: v4_ref[r4, krows, :],
                              b == 0, (stat4_ref, (0, r4, rows))))
    for r16 in range(16):
        r4, sub = r16 % 4, r16 // 4
        rows = pl.ds(sub, BLK, stride=4)
        krows = pl.ds(sub, 2 * BLK, stride=4)
        blocks[2].append((lambda r4=r4, rows=rows: q4_ref[r4, rows, :],
                          lambda r4=r4, krows=krows: k4_ref[r4, krows, :],
                          lambda r4=r4, krows=krows: v4_ref[r4, krows, :],
                          True, (stat4_ref, (1, r4, rows))))

    def split_heads(x):
        return jnp.where(head_a, x[:BLK], x[BLK:])

    def scores(pat, c):
        load_q, load_k = blocks[pat][c][:2]
        q = (load_q() * (HEAD_DIM ** -0.5)).astype(BF16)
        zero = jnp.zeros_like(q)
        q2 = jnp.concatenate([jnp.where(head_a, q, zero), jnp.where(head_a, zero, q)], axis=0)
        s_ref[pat, c] = lax.dot_general(q2, load_k().astype(BF16), dims_nt,
                                        preferred_element_type=F32)

    def softmax_pv(pat, c):
        _, _, load_v, first, (stat_ref, where) = blocks[pat][c]
        s = s_ref[pat, c] + bias_ref[1 if first else 0]
        m = jnp.max(jnp.maximum(s[:, :BLK], s[:, BLK:]), axis=1, keepdims=True)
        p = jnp.exp(s - m)
        l = jnp.sum(p[:, :BLK] + p[:, BLK:], axis=1, keepdims=True)
        pv2 = jnp.dot(p.astype(BF16), load_v().astype(BF16), preferred_element_type=F32)
        stat_ref[(0,) + where] = split_heads(pv2)
        stat_ref[(1,) + where] = split_heads(jnp.broadcast_to(m, (HEADS_PER_STEP * BLK, LANES)))
        stat_ref[(2,) + where] = split_heads(jnp.broadcast_to(l, (HEADS_PER_STEP * BLK, LANES)))

    order = [(pat, c) for pat in range(len(blocks)) for c in range(len(blocks[pat]))]
    for pos in range(len(order) + ATTN_AHEAD):
        if pos < len(order):
            scores(*order[pos])
        if pos >= ATTN_AHEAD:
            softmax_pv(*order[pos - ATTN_AHEAD])

    for r4 in range(4):
        by4 = pl.ds(r4, quarter, stride=4)
        num, mx, den = zip((stat1_ref[0, by4, :], stat1_ref[1, by4, :], stat1_ref[2, by4, :]),
                           *[(stat4_ref[0, pat, r4], stat4_ref[1, pat, r4], stat4_ref[2, pat, r4])
                             for pat in range(2)])
        m_max = functools.reduce(jnp.maximum, mx)
        w = [jnp.exp(m - m_max) for m in mx]
        o4_ref[by4, :] = (sum(wi * ni for wi, ni in zip(w, num))
                          / sum(wi * di for wi, di in zip(w, den)))
    o_ref[...] = o4_ref[...].astype(o_ref.dtype)


def _attn(qkv, batch, seq):
    ns = seq // SUPER
    hg = D_ATTN // LANES
    blk = lambda imap: pl.BlockSpec((None, SUPER, LANES), imap)
    return pl.pallas_call(
        _attn_kernel,
        grid=(batch, hg, ns),
        in_specs=[
            blk(lambda b, g, n: (b, n, g)),
            blk(lambda b, g, n: (b, jnp.maximum(n - 1, 0), hg + g)),
            blk(lambda b, g, n: (b, n, hg + g)),
            blk(lambda b, g, n: (b, jnp.maximum(n - 1, 0), 2 * hg + g)),
            blk(lambda b, g, n: (b, n, 2 * hg + g)),
        ],
        out_specs=blk(lambda b, g, n: (b, n, g)),
        out_shape=jax.ShapeDtypeStruct((batch, seq, D_ATTN), BF16),
        scratch_shapes=[
            pltpu.VMEM((3, SUPER, LANES), F32),
            pltpu.VMEM((3, 2, 4, SUPER // 4, LANES), F32),
            pltpu.VMEM((4, SUPER // 4, LANES), F32),
            pltpu.VMEM((4, 2 * SUPER // 4, LANES), F32),
            pltpu.VMEM((4, 2 * SUPER // 4, LANES), F32),
            pltpu.VMEM((len(DILATIONS), SUPER // BLK, HEADS_PER_STEP * BLK, 2 * BLK), F32),
            pltpu.VMEM((2, HEADS_PER_STEP * BLK, 2 * BLK), F32),
            pltpu.VMEM((SUPER, LANES), F32)],
        compiler_params=pltpu.CompilerParams(
            dimension_semantics=("parallel", "parallel", "arbitrary"),
            vmem_limit_bytes=VMEM_LIMIT),
        name="attn",
    )(qkv, qkv, qkv, qkv, qkv)


def _outproj_kernel(x_ref, a_ref, cp_ref, w_ref, o_ref):
    o_ref[...] = (x_ref[...]
                  + jnp.dot(a_ref[...], w_ref[0:D_ATTN, :], preferred_element_type=F32)
                  + jnp.dot(cp_ref[...], w_ref[D_ATTN:, :], preferred_element_type=F32))


def _outproj(x, y_attn, y_cp, layer, w_out):
    t = x.shape[0]
    tm = OUTPROJ_TM
    return pl.pallas_call(
        _outproj_kernel,
        grid=(t // tm,),
        in_specs=[
            pl.BlockSpec((tm, D_MODEL), lambda i: (i, 0)),
            pl.BlockSpec((tm, D_ATTN), lambda i: (i, 0)),
            pl.BlockSpec((tm, D_CP), lambda i: (i, 0)),
            pl.BlockSpec((None, D_MODEL, D_MODEL), lambda i: (layer, 0, 0),
                         pipeline_mode=pl.Buffered(1)),
        ],
        out_specs=pl.BlockSpec((tm, D_MODEL), lambda i: (i, 0)),
        out_shape=jax.ShapeDtypeStruct((t, D_MODEL), F32),
        compiler_params=pltpu.CompilerParams(
            dimension_semantics=("parallel",), vmem_limit_bytes=VMEM_LIMIT),
        name="outproj",
    )(x, y_attn, y_cp, w_out)


def kernel(x, ffn1_norm, ffn1_w_gate, ffn1_w_up, ffn1_w_down, mix_norm, w_in, conv_w, pool_w, pool_scale, w_out, ffn2_norm, ffn2_w_gate, ffn2_w_up, ffn2_w_down, final_norm):
    batch, seq, _ = x.shape
    depth = w_in.shape[0]
    assert seq % SUPER == 0 and seq % INPROJ_TM == 0 and (batch * seq) % FFN_TM == 0
    t = batch * seq
    x = x.reshape(t, D_MODEL)
    ffn1 = (ffn1_norm, ffn1_w_gate, ffn1_w_up, ffn1_w_down)
    ffn2 = (ffn2_norm, ffn2_w_gate, ffn2_w_up, ffn2_w_down)
    w_in, pool_w, w_out = w_in.astype(BF16), pool_w.astype(BF16), w_out.astype(BF16)
    for l in range(depth):
        x = _ffn(x, l, *ffn1)
        qkv, y_cp = _inproj(x, l, mix_norm, w_in, conv_w, pool_w, pool_scale, seq)
        y_attn = _attn(qkv.reshape(batch, seq, 3 * D_ATTN), batch, seq)
        x = _outproj(x, y_attn.reshape(t, D_ATTN), y_cp, l, w_out)
        x = _ffn(x, l, *ffn2, final_g=final_norm if l == depth - 1 else None)
    return x.reshape(batch, seq, D_MODEL)
```
